```python
import math
import jax
import jax.numpy as jnp
from jax import lax
import numpy as np

D_MODEL = 4096
BATCH = 4
SEQ = 4096
DEPTH = 2
DEC_BATCH = 16
DEC_SEQ = 16
PAST_LEN = 2048

CHUNK = 64
MIX_WIDTH = D_MODEL
DN_WIDTH = MIX_WIDTH // 2
DN_HEAD_DIM = 128
DN_HEADS = DN_WIDTH // DN_HEAD_DIM
CONV_W = 4
DN_CONV_DIM = 3 * DN_WIDTH
ATT_WIDTH = MIX_WIDTH - DN_WIDTH
ATT_HEAD_DIM = 64
ATT_Q_HEADS = ATT_WIDTH // ATT_HEAD_DIM
ATT_KV_HEADS = ATT_Q_HEADS // 8
ATT_GROUP = ATT_Q_HEADS // ATT_KV_HEADS
ATT_KV_WIDTH = ATT_KV_HEADS * ATT_HEAD_DIM
WINDOW = 128
ROPE_THETA = 10000.0
NORM_EPS = 1e-6
IN_SIZES = (DN_CONV_DIM, DN_WIDTH, DN_HEADS, DN_HEADS, ATT_WIDTH, ATT_KV_WIDTH, ATT_KV_WIDTH, ATT_WIDTH)
IN_DIM = sum(IN_SIZES)

kernel_name = 'hybrid_gdn_swa_sink_stream_step'


def rms_norm(x, gain):
    xf = x.astype(jnp.float32)
    y = xf * lax.rsqrt(jnp.mean(xf * xf, axis=-1, keepdims=True) + NORM_EPS)
    return (y * gain.astype(jnp.float32)).astype(x.dtype)


def l2_norm(x):
    xf = x.astype(jnp.float32)
    return xf * lax.rsqrt(jnp.sum(xf * xf, axis=-1, keepdims=True) + NORM_EPS)


def rope(x, pos):
    half = x.shape[-1] // 2
    inv = ROPE_THETA ** (-jnp.arange(half, dtype=jnp.float32) / half)
    ang = pos.astype(jnp.float32)[:, None] * inv[None, :]
    cos = jnp.cos(ang)[None, :, None, :]
    sin = jnp.sin(ang)[None, :, None, :]
    x1 = x[..., :half].astype(jnp.float32)
    x2 = x[..., half:].astype(jnp.float32)
    return jnp.concatenate([x1 * cos - x2 * sin, x2 * cos + x1 * sin], axis=-1).astype(x.dtype)


def split_cols(p):
    points = []
    acc = 0
    for s in IN_SIZES[:-1]:
        acc += s
        points.append(acc)
    return jnp.split(p, points, axis=-1)


def causal_short_conv(x, buf, w):
    t = x.shape[1]
    xp = jnp.concatenate([buf.astype(x.dtype), x], axis=1)
    y = xp[:, 0:t] * w[0]
    for j in range(1, CONV_W):
        y = y + xp[:, j:j + t] * w[j]
    return jax.nn.silu(y), xp[:, -(CONV_W - 1):]


def gated_delta_rule(q, k, v, g, beta, s0):
    n, t, h, dk = q.shape
    dv = v.shape[-1]
    cs = min(CHUNK, t)
    nc = t // cs

    def blk(a):
        return a.reshape(n, nc, cs, *a.shape[2:])

    q, k, v, g, beta = blk(q) * dk ** -0.5, blk(k), blk(v), blk(g), blk(beta)
    gc = jnp.cumsum(g, axis=2)
    idx = jnp.arange(cs)
    causal = (idx[:, None] >= idx[None, :])[None, None, :, :, None]
    strict = (idx[:, None] > idx[None, :])[None, None, :, :, None]
    diff = gc[:, :, :, None, :] - gc[:, :, None, :, :]
    gamma = jnp.exp(jnp.where(causal, diff, -jnp.inf))
    kk = jnp.einsum('ncihd,ncjhd->ncijh', k, k)
    a_mat = jnp.where(strict, beta[:, :, :, None, :] * kk * gamma, 0.0).transpose(0, 1, 4, 2, 3)
    rhs_u = (beta[..., None] * v).transpose(0, 1, 3, 2, 4)
    rhs_w = (beta[..., None] * jnp.exp(gc)[..., None] * k).transpose(0, 1, 3, 2, 4)
    sol = lax.linalg.triangular_solve(a_mat, jnp.concatenate([rhs_u, rhs_w], axis=-1),
                                      left_side=True, lower=True, unit_diagonal=True)
    u, w = sol[..., :dv], sol[..., dv:]
    qk = jnp.einsum('ncihd,ncjhd->nchij', q, k) * gamma.transpose(0, 1, 4, 2, 3)
    qdec = (q * jnp.exp(gc)[..., None]).transpose(0, 1, 3, 2, 4)
    kdec = (k * jnp.exp(gc[:, :, -1:, :] - gc)[..., None]).transpose(0, 1, 3, 2, 4)
    blk_decay = jnp.exp(gc[:, :, -1, :])
    xs = tuple(a.swapaxes(0, 1) for a in (u, w, qk, qdec, kdec, blk_decay))

    def step(s, inp):
        u_c, w_c, qk_c, qdec_c, kdec_c, dec_c = inp
        v_new = u_c - jnp.einsum('nhid,nhde->nhie', w_c, s)
        o = jnp.einsum('nhid,nhde->nhie', qdec_c, s) + jnp.einsum('nhij,nhje->nhie', qk_c, v_new)
        s = s * dec_c[..., None, None] + jnp.einsum('nhjd,nhje->nhde', kdec_c, v_new)
        return s, o

    s_final, o = lax.scan(step, s0, xs)
    o = o.transpose(1, 0, 3, 2, 4).reshape(n, t, h, dv)
    return o, s_final


def deltanet_branch(qkv_pre, z, b, a, conv_buf, s0, conv_w, a_log, dt_bias, norm_g):
    n, t, _ = qkv_pre.shape
    qkv, new_buf = causal_short_conv(qkv_pre, conv_buf, conv_w)
    q, k, v = jnp.split(qkv, 3, axis=-1)
    q = l2_norm(q.reshape(n, t, DN_HEADS, DN_HEAD_DIM))
    k = l2_norm(k.reshape(n, t, DN_HEADS, DN_HEAD_DIM))
    v = v.reshape(n, t, DN_HEADS, DN_HEAD_DIM).astype(jnp.float32)
    beta = jax.nn.sigmoid(b.astype(jnp.float32))
    g = -jnp.exp(a_log.astype(jnp.float32)) * jax.nn.softplus(a.astype(jnp.float32) + dt_bias.astype(jnp.float32))
    o, s_new = gated_delta_rule(q, k, v, g, beta, s0.astype(jnp.float32))
    zg = jax.nn.silu(z.reshape(n, t, DN_HEADS, DN_HEAD_DIM).astype(jnp.float32))
    o = rms_norm(o, norm_g) * zg
    return o.reshape(n, t, DN_WIDTH).astype(qkv_pre.dtype), new_buf, s_new


def sink_attention(q, k, v, sinks, mask):
    s = jnp.einsum('...tkgd,...skd->...kgts', q, k).astype(jnp.float32) * ATT_HEAD_DIM ** -0.5
    if mask is not None:
        s = jnp.where(mask, s, -jnp.inf)
    sink = sinks.astype(jnp.float32).reshape(ATT_KV_HEADS, ATT_GROUP)[:, :, None, None]
    m = jnp.maximum(jnp.max(s, axis=-1, keepdims=True), sink)
    p = jnp.exp(s - m)
    p = (p / (jnp.sum(p, axis=-1, keepdims=True) + jnp.exp(sink - m))).astype(v.dtype)
    return jnp.einsum('...kgts,...skd->...tkgd', p, v)


def to_band(x, nc, nb):
    n = x.shape[0]
    xp = jnp.pad(x, ((0, 0), (nb * CHUNK, 0), (0, 0), (0, 0)))
    xp = xp.reshape(n, nc + nb, CHUNK, *x.shape[2:])
    return jnp.concatenate([xp[:, j:j + nc] for j in range(nb + 1)], axis=2)


def swa_branch(q, k, v, z, pos, sinks, cache_k, cache_v):
    n, t, _ = q.shape
    q = rope(q.reshape(n, t, ATT_Q_HEADS, ATT_HEAD_DIM), pos)
    k = rope(k.reshape(n, t, ATT_KV_HEADS, ATT_HEAD_DIM), pos)
    v = v.reshape(n, t, ATT_KV_HEADS, ATT_HEAD_DIM)
    if cache_k is None:
        nc = t // CHUNK
        nb = WINDOW // CHUNK
        band = (nb + 1) * CHUNK
        qb = q.reshape(n, nc, CHUNK, ATT_KV_HEADS, ATT_GROUP, ATT_HEAD_DIM)
        kpos = (jnp.arange(nc)[:, None] - nb) * CHUNK + jnp.arange(band)[None, :]
        mask = (kpos >= 0)[None, :, None, None, None, :]
        o = sink_attention(qb, to_band(k, nc, nb), to_band(v, nc, nb), sinks, mask)
        k_all, v_all = k, v
        rows = WINDOW
    else:
        k_all = jnp.concatenate([cache_k.astype(k.dtype), k], axis=1)
        v_all = jnp.concatenate([cache_v.astype(v.dtype), v], axis=1)
        qg = q.reshape(n, t, ATT_KV_HEADS, ATT_GROUP, ATT_HEAD_DIM)
        o = sink_attention(qg, k_all, v_all, sinks, None)
        rows = cache_k.shape[1]
    o = o.reshape(n, t, ATT_WIDTH) * jax.nn.silu(z)
    return o, k_all[:, -rows:], v_all[:, -rows:]


def layer(x, c, pos, conv_buf, dn_state, win_k, win_v,
          w_ada, b_ada, g_pre, g_post, w_in, conv_w, a_log, dt_bias, dn_norm, sinks, w_out):
    shift, scale, gate = jnp.split(jax.nn.silu(c) @ w_ada + b_ada, 3, axis=-1)
    h = rms_norm(x, g_pre) * (1.0 + scale[:, None, :]) + shift[:, None, :]
    qkv_pre, z_dn, b_dn, a_dn, q_at, k_at, v_at, z_at = split_cols(h @ w_in)
    o_dn, conv_new, s_new = deltanet_branch(qkv_pre, z_dn, b_dn, a_dn, conv_buf, dn_state,
                                            conv_w, a_log, dt_bias, dn_norm)
    o_at, k_new, v_new = swa_branch(q_at, k_at, v_at, z_at, pos, sinks, win_k, win_v)
    y = jnp.concatenate([o_dn, o_at], axis=-1) @ w_out
    x = x + gate[:, None, :] * rms_norm(y, g_post)
    return x, conv_new, s_new.astype(x.dtype), k_new, v_new


def setup_inputs(seed: int = 0) -> dict:
    key = jax.random.key(seed)
    ks = jax.random.split(key, 22)
    f32 = jnp.float32

    def nrm(k, shape, s):
        return jax.random.normal(k, shape, f32) * s

    win_rows = min(WINDOW, PAST_LEN)
    dt = jnp.exp(jax.random.uniform(ks[15], (DEPTH, DN_HEADS), f32, math.log(1e-3), math.log(1e-1)))
    return {
        'x_prompt': nrm(ks[0], (BATCH, SEQ, D_MODEL), 1.0),
        'x_sample': nrm(ks[1], (DEC_BATCH, DEC_SEQ, D_MODEL), 1.0),
        'state_conv': nrm(ks[2], (DEPTH, DEC_BATCH, CONV_W - 1, DN_CONV_DIM), 1.0),
        'state_dn': nrm(ks[3], (DEPTH, DEC_BATCH, DN_HEADS, DN_HEAD_DIM, DN_HEAD_DIM), 0.05),
        'cache_k': nrm(ks[4], (DEPTH, DEC_BATCH, win_rows, ATT_KV_HEADS, ATT_HEAD_DIM), 1.0),
        'cache_v': nrm(ks[5], (DEPTH, DEC_BATCH, win_rows, ATT_KV_HEADS, ATT_HEAD_DIM), 1.0),
        'c_prompt': nrm(ks[6], (BATCH, D_MODEL), 1.0),
        'c_sample': nrm(ks[7], (DEC_BATCH, D_MODEL), 1.0),
        'w_ada': nrm(ks[8], (DEPTH, D_MODEL, 3 * D_MODEL), 0.5 * D_MODEL ** -0.5),
        'b_ada': nrm(ks[9], (DEPTH, 3 * D_MODEL), 0.02),
        'g_pre': 1.0 + nrm(ks[10], (DEPTH, D_MODEL), 0.05),
        'g_post': 1.0 + nrm(ks[11], (DEPTH, D_MODEL), 0.05),
        'w_in': nrm(ks[12], (DEPTH, D_MODEL, IN_DIM), D_MODEL ** -0.5),
        'conv_w': nrm(ks[13], (DEPTH, CONV_W, DN_CONV_DIM), CONV_W ** -0.5),
        'a_log': jnp.log(jax.random.uniform(ks[14], (DEPTH, DN_HEADS), f32, 1.0, 16.0)),
        'dt_bias': dt + jnp.log(-jnp.expm1(-dt)),
        'dn_norm': 1.0 + nrm(ks[16], (DEPTH, DN_HEAD_DIM), 0.05),
        'sinks': nrm(ks[17], (DEPTH, ATT_Q_HEADS), 0.5),
        'w_out': nrm(ks[18], (DEPTH, MIX_WIDTH, D_MODEL), MIX_WIDTH ** -0.5),
    }


def reference(x_prompt, x_sample, state_conv, state_dn, cache_k, cache_v, c_prompt, c_sample,
              w_ada, b_ada, g_pre, g_post, w_in, conv_w, a_log, dt_bias, dn_norm, sinks, w_out):
    bp, tp, _ = x_prompt.shape
    pos_p = jnp.arange(tp)
    pos_s = PAST_LEN + jnp.arange(x_sample.shape[1])
    zero_conv = jnp.zeros((bp, CONV_W - 1, DN_CONV_DIM), x_prompt.dtype)
    zero_state = jnp.zeros((bp, DN_HEADS, DN_HEAD_DIM, DN_HEAD_DIM), jnp.float32)
    xp, xs = x_prompt, x_sample
    conv_p, dn_p, k_p, v_p = [], [], [], []
    conv_s, dn_s, k_s, v_s = [], [], [], []
    for l in range(DEPTH):
        lw = (w_ada[l], b_ada[l], g_pre[l], g_post[l], w_in[l], conv_w[l],
              a_log[l], dt_bias[l], dn_norm[l], sinks[l], w_out[l])
        xp, cb, sb, kb, vb = layer(xp, c_prompt, pos_p, zero_conv, zero_state, None, None, *lw)
        conv_p.append(cb); dn_p.append(sb); k_p.append(kb); v_p.append(vb)
        xs, cb, sb, kb, vb = layer(xs, c_sample, pos_s, state_conv[l], state_dn[l],
                                   cache_k[l], cache_v[l], *lw)
        conv_s.append(cb); dn_s.append(sb); k_s.append(kb); v_s.append(vb)
    return (xp, xs,
            jnp.stack(conv_p), jnp.stack(dn_p), jnp.stack(k_p), jnp.stack(v_p),
            jnp.stack(conv_s), jnp.stack(dn_s), jnp.stack(k_s), jnp.stack(v_s))
```

```python
import functools
import math

import jax
import jax.numpy as jnp
from jax import lax
from jax.experimental import pallas as pl
from jax.experimental.pallas import tpu as pltpu

F32 = jnp.float32
BF16 = jnp.bfloat16

D_MODEL = 4096
CHUNK = 64
DN_WIDTH = D_MODEL // 2
DN_HEAD_DIM = 128
DN_HEADS = DN_WIDTH // DN_HEAD_DIM
CONV_W = 4
DN_CONV_DIM = 3 * DN_WIDTH
ATT_WIDTH = D_MODEL - DN_WIDTH
ATT_HEAD_DIM = 64
ATT_Q_HEADS = ATT_WIDTH // ATT_HEAD_DIM
ATT_KV_HEADS = ATT_Q_HEADS // 8
ATT_KV_WIDTH = ATT_KV_HEADS * ATT_HEAD_DIM
WINDOW = 128
ROPE_THETA = 10000.0
NORM_EPS = 1e-6
PAST_LEN = 2048

LANES = 128
SUBLANES = 8
VMEM_LIMIT_BYTES = 56 * 1024 * 1024

COL_QKV = 0
COL_ZDN = COL_QKV + DN_CONV_DIM
COL_QAT = COL_ZDN + DN_WIDTH
COL_ZAT = COL_QAT + ATT_WIDTH
COL_KAT = COL_ZAT + ATT_WIDTH
COL_VAT = COL_KAT + ATT_KV_WIDTH
MAIN_DIM = COL_VAT + ATT_KV_WIDTH
BA_DIM = LANES

ROW_CHUNK = 16
PAIR = LANES // ATT_HEAD_DIM
N_PAIRS = ATT_Q_HEADS // PAIR
KEY_SLOTS = 4
KEYS_PAD = KEY_SLOTS * CHUNK


def _cparams(sem):
    return pltpu.CompilerParams(dimension_semantics=sem, vmem_limit_bytes=VMEM_LIMIT_BYTES)


def _dot(a, b):
    return jnp.dot(a, b, preferred_element_type=F32)


def _dot_nt(a, b):
    return lax.dot_general(a, b, (((1,), (1,)), ((), ())), preferred_element_type=F32)


def _dot_tn(a, b):
    return lax.dot_general(a, b, (((0,), (0,)), ((), ())), preferred_element_type=F32)


def _split(a):
    hi = a.astype(BF16)
    lo = (a - hi.astype(F32)).astype(BF16)
    return hi, lo


def _dot_split(a, b):
    ah, al = _split(a)
    bh, bl = _split(b)
    return _dot(ah, bh) + (_dot(ah, bl) + _dot(al, bh))


def _sigmoid(x):
    return 1.0 / (1.0 + jnp.exp(-x))


def _silu(x):
    return x * _sigmoid(x)


def _ada_kernel(c_ref, w_ref, b_ref, o_ref):
    s = _silu(c_ref[...]).astype(BF16)
    o_ref[...] = _dot(s, w_ref[...].astype(BF16)) + b_ref[...]


def _ada(c_all, w_ada, b_ada):
    rows, d = c_all.shape
    n_out = w_ada.shape[1]
    tn = 512
    return pl.pallas_call(
        _ada_kernel,
        grid=(n_out // tn,),
        in_specs=[
            pl.BlockSpec((rows, d), lambda j: (0, 0)),
            pl.BlockSpec((d, tn), lambda j: (0, j)),
            pl.BlockSpec((1, tn), lambda j: (0, j)),
        ],
        out_specs=pl.BlockSpec((rows, tn), lambda j: (0, j)),
        out_shape=jax.ShapeDtypeStruct((rows, n_out), F32),
        compiler_params=_cparams(("arbitrary",)),
        name="ada",
    )(c_all, w_ada, b_ada.reshape(1, n_out))


def _in_proj_kernel(x_ref, mod_ref, g_ref, w_ref, wba_ref, o_ref, oba_ref, h_ref, *, nb, tt):
    j = pl.program_id(1)
    chunks_per_batch = tt // ROW_CHUNK

    @pl.when(j == 0)
    def _():
        def body(r, carry):
            b = r // chunks_per_batch
            r0 = pl.multiple_of((r % chunks_per_batch) * ROW_CHUNK, ROW_CHUNK)
            x = x_ref[b, pl.ds(r0, ROW_CHUNK), :]
            y = x * lax.rsqrt(jnp.mean(x * x, axis=-1, keepdims=True) + NORM_EPS) * g_ref[...]
            mod = mod_ref[b]
            h = y * (1.0 + mod[1:2, :]) + mod[0:1, :]
            h_ref[pl.ds(pl.multiple_of(r * ROW_CHUNK, ROW_CHUNK), ROW_CHUNK), :] = h.astype(BF16)
            return carry

        lax.fori_loop(0, nb * chunks_per_batch, body, 0)
        oba_ref[...] = _dot(h_ref[...], wba_ref[...])

    o_ref[...] = _dot(h_ref[...], w_ref[...])


def _in_proj(x, mod, g_pre, w_main, w_ba, *, nb, tt, tn):
    n, t, d = x.shape
    tm = nb * tt
    row_tiles = (n // nb) * (t // tt)
    tiles_per_batch = t // tt
    return pl.pallas_call(
        functools.partial(_in_proj_kernel, nb=nb, tt=tt),
        grid=(row_tiles, MAIN_DIM // tn),
        in_specs=[
            pl.BlockSpec((nb, tt, d), lambda i, j: (i // tiles_per_batch, i % tiles_per_batch, 0)),
            pl.BlockSpec((nb, 3, d), lambda i, j: (i // tiles_per_batch, 0, 0)),
            pl.BlockSpec((1, d), lambda i, j: (0, 0)),
            pl.BlockSpec((d, tn), lambda i, j: (0, j)),
            pl.BlockSpec((d, BA_DIM), lambda i, j: (0, 0)),
        ],
        out_specs=[
            pl.BlockSpec((tm, tn), lambda i, j: (i, j)),
            pl.BlockSpec((tm, BA_DIM), lambda i, j: (i, 0)),
        ],
        out_shape=[
            jax.ShapeDtypeStruct((n * t, MAIN_DIM), F32),
            jax.ShapeDtypeStruct((n * t, BA_DIM), F32),
        ],
        scratch_shapes=[pltpu.VMEM((tm, d), BF16)],
        compiler_params=_cparams(("arbitrary", "arbitrary")),
        name="in_proj",
    )(x, mod, g_pre.reshape(1, d), w_main, w_ba)


def _deltanet_kernel(qkv_ref, z_ref, ba_ref, cbuf_ref, cw_ref, gpar_ref, ng_ref, s0_ref,
                     o_ref, s_ref, ext_ref, gct_ref, *, cs, head_group):
    c = pl.program_id(1)
    hd = DN_HEAD_DIM

    @pl.when(c == 0)
    def _():
        s_ref[...] = s0_ref[...]
        ext_ref[0:SUBLANES, :] = cbuf_ref[0]

    ext_ref[SUBLANES:SUBLANES + cs, :] = qkv_ref[...]

    ba = ba_ref[...]
    beta_all = _sigmoid(ba)
    xg = ba + gpar_ref[1:2, :]
    softplus = jnp.maximum(xg, 0.0) + jnp.log1p(jnp.exp(-jnp.abs(xg)))
    g_all = -jnp.exp(gpar_ref[0:1, :]) * softplus
    row = lax.broadcasted_iota(jnp.int32, (cs, LANES), 0)
    gc_all = g_all
    shift = 1
    while shift < cs:
        gc_all = gc_all + jnp.where(row >= shift, pltpu.roll(gc_all, shift, axis=0), 0.0)
        shift *= 2
    gct_ref[...] = gc_all.T
    lane = lax.broadcasted_iota(jnp.int32, (cs, LANES), 1)

    ri = lax.broadcasted_iota(jnp.int32, (cs, cs), 0)
    ci = lax.broadcasted_iota(jnp.int32, (cs, cs), 1)
    causal = ri >= ci
    strict = ri > ci
    eye = jnp.where(ri == ci, 1.0, 0.0).astype(F32)
    n_doublings = int(math.log2(cs)) - 1

    def conv_silu(col0):
        col0 = pl.multiple_of(col0, hd)
        acc = ext_ref[pl.ds(SUBLANES - (CONV_W - 1), cs), pl.ds(col0, hd)] * cw_ref[0:1, pl.ds(col0, hd)]
        for j in range(1, CONV_W):
            acc = acc + (ext_ref[pl.ds(SUBLANES - (CONV_W - 1) + j, cs), pl.ds(col0, hd)]
                         * cw_ref[j:j + 1, pl.ds(col0, hd)])
        return _silu(acc)

    def one_head(h):
        off = pl.multiple_of(h * hd, hd)
        q = conv_silu(off)
        k = conv_silu(DN_WIDTH + off)
        v = conv_silu(2 * DN_WIDTH + off)
        q = q * lax.rsqrt(jnp.sum(q * q, axis=-1, keepdims=True) + NORM_EPS) * hd ** -0.5
        k = k * lax.rsqrt(jnp.sum(k * k, axis=-1, keepdims=True) + NORM_EPS)

        beta = jnp.sum(jnp.where(lane == h, beta_all, 0.0), axis=-1, keepdims=True)
        gcol = jnp.sum(jnp.where(lane == DN_HEADS + h, gc_all, 0.0), axis=-1, keepdims=True)
        grow = gct_ref[pl.ds(DN_HEADS + h, 1), :]
        glast = grow[:, cs - 1:cs]
        gamma = jnp.where(causal, jnp.exp(gcol - grow), 0.0)
        ecol = jnp.exp(gcol)

        kb = k.astype(BF16)
        kq = _dot_nt(jnp.concatenate([kb, q.astype(BF16)], axis=0), kb)
        a_mat = jnp.where(strict, beta * kq[:cs] * gamma, 0.0)
        qk = kq[cs:] * gamma

        inv = eye - a_mat
        apow = a_mat.astype(BF16)
        for _ in range(n_doublings):
            apow_f = _dot(apow, apow)
            apow = apow_f.astype(BF16)
            inv = inv + _dot(inv.astype(BF16), apow)
        resid = eye - inv - _dot_split(a_mat, inv)
        inv = inv + _dot(inv.astype(BF16), resid.astype(BF16))

        rhs = jnp.concatenate([beta * v, beta * ecol * k], axis=-1).astype(BF16)
        uw = _dot(inv.astype(BF16), rhs)
        u = uw[:, :hd]
        w = uw[:, hd:]

        state = s_ref[0, h]
        qdec = q * ecol
        ws = _dot(jnp.concatenate([w, qdec], axis=0).astype(BF16), state.astype(BF16))
        v_new = u - ws[:cs]
        v_new_b = v_new.astype(BF16)
        o = ws[cs:] + _dot(qk.astype(BF16), v_new_b)
        kdec = k * jnp.exp(glast - gcol)
        s_ref[0, h] = state * jnp.exp(glast) + _dot_tn(kdec.astype(BF16), v_new_b)

        o = o * lax.rsqrt(jnp.mean(o * o, axis=-1, keepdims=True) + NORM_EPS) * ng_ref[...]
        o = o * _silu(z_ref[:, pl.ds(off, hd)])
        o_ref[:, pl.ds(off, hd)] = o.astype(o_ref.dtype)

    def group(gi, carry):
        for hh in range(head_group):
            one_head(gi * head_group + hh)
        return carry

    lax.fori_loop(0, DN_HEADS // head_group, group, 0)
    ext_ref[0:SUBLANES, :] = ext_ref[cs:cs + SUBLANES, :]


def _deltanet(main, ba, conv_buf8, conv_w, gpar, dn_norm, s0, *, n, t, cs, head_group=4):
    nc = t // cs
    hd = DN_HEAD_DIM
    return pl.pallas_call(
        functools.partial(_deltanet_kernel, cs=cs, head_group=head_group),
        grid=(n, nc),
        in_specs=[
            pl.BlockSpec((cs, DN_CONV_DIM), lambda b, c: (b * nc + c, COL_QKV // DN_CONV_DIM)),
            pl.BlockSpec((cs, DN_WIDTH), lambda b, c: (b * nc + c, COL_ZDN // DN_WIDTH)),
            pl.BlockSpec((cs, BA_DIM), lambda b, c: (b * nc + c, 0)),
            pl.BlockSpec((1, SUBLANES, DN_CONV_DIM), lambda b, c: (b, 0, 0)),
            pl.BlockSpec((CONV_W, DN_CONV_DIM), lambda b, c: (0, 0)),
            pl.BlockSpec((2, BA_DIM), lambda b, c: (0, 0)),
            pl.BlockSpec((1, hd), lambda b, c: (0, 0)),
            pl.BlockSpec((1, DN_HEADS, hd, hd), lambda b, c: (b, 0, 0, 0)),
        ],
        out_specs=[
            pl.BlockSpec((cs, DN_WIDTH), lambda b, c: (b * nc + c, 0)),
            pl.BlockSpec((1, DN_HEADS, hd, hd), lambda b, c: (b, 0, 0, 0)),
        ],
        out_shape=[
            jax.ShapeDtypeStruct((n * t, DN_WIDTH), BF16),
            jax.ShapeDtypeStruct((n, DN_HEADS, hd, hd), F32),
        ],
        scratch_shapes=[
            pltpu.VMEM((SUBLANES + cs, DN_CONV_DIM), F32),
            pltpu.VMEM((LANES, cs), F32),
        ],
        compiler_params=_cparams(("arbitrary", "arbitrary")),
        name="deltanet",
    )(main, main, ba, conv_buf8, conv_w, gpar, dn_norm.reshape(1, hd), s0)


def _rope(x, cos, sin_signed):
    lane = lax.broadcasted_iota(jnp.int32, x.shape, 1)
    half = ATT_HEAD_DIM // 2
    swapped = jnp.where((lane % ATT_HEAD_DIM) < half,
                        pltpu.roll(x, LANES - half, axis=1), pltpu.roll(x, half, axis=1))
    return x * cos + swapped * sin_signed


def _split_heads(tile):
    lane = lax.broadcasted_iota(jnp.int32, tile.shape, 1)
    lo = lane < ATT_HEAD_DIM
    rolled = pltpu.roll(tile, ATT_HEAD_DIM, axis=1)
    zero = jnp.zeros_like(tile)
    return (jnp.where(lo, tile, zero), jnp.where(lo, zero, rolled),
            jnp.where(lo, rolled, zero), jnp.where(lo, zero, tile))


def _store_keys(dst_ref, tile_idx, row0, rows, tile):
    a_top, a_bot, b_top, b_bot = _split_heads(tile)
    ja = PAIR * tile_idx
    dst_ref[ja, pl.ds(row0, rows), :] = a_top.astype(BF16)
    dst_ref[ja, pl.ds(KEYS_PAD + row0, rows), :] = a_bot.astype(BF16)
    dst_ref[ja + 1, pl.ds(row0, rows), :] = b_top.astype(BF16)
    dst_ref[ja + 1, pl.ds(KEYS_PAD + row0, rows), :] = b_bot.astype(BF16)


def _attend_pairs(q_ref, z_ref, cos, sin_signed, sinks_ref, k2_ref, v2_ref, o_ref, valid, unroll):
    rows = q_ref.shape[0]
    lane = lax.broadcasted_iota(jnp.int32, (rows, LANES), 1)
    lo = lane < ATT_HEAD_DIM
    neg_inf = jnp.full((rows, PAIR * KEYS_PAD), -jnp.inf, F32)

    def pair(p, carry):
        off = pl.multiple_of(p * LANES, LANES)
        kv = p // (N_PAIRS // ATT_KV_HEADS)
        q = _rope(q_ref[:, pl.ds(off, LANES)], cos, sin_signed) * ATT_HEAD_DIM ** -0.5
        s = _dot_nt(q.astype(BF16), k2_ref[kv])
        s = jnp.where(valid, s, neg_inf)
        probs = []
        inv_den = []
        for hh in range(PAIR):
            sh = s[:, hh * KEYS_PAD:(hh + 1) * KEYS_PAD]
            sink = sinks_ref[PAIR * p + hh]
            m = jnp.maximum(jnp.max(sh, axis=-1, keepdims=True), sink)
            ph = jnp.exp(sh - m)
            den = jnp.sum(ph, axis=-1, keepdims=True) + jnp.exp(sink - m)
            probs.append(ph)
            inv_den.append(1.0 / den)
        o = _dot(jnp.concatenate(probs, axis=-1).astype(BF16), v2_ref[kv])
        o = o * jnp.where(lo, inv_den[0], inv_den[1])
        o = o * _silu(z_ref[:, pl.ds(off, LANES)])
        o_ref[:, pl.ds(off, LANES)] = o.astype(o_ref.dtype)
        return carry

    lax.fori_loop(0, N_PAIRS, pair, 0, unroll=unroll)


def _swa_prompt_kernel(sinks_ref, q_ref, z_ref, k_ref, v_ref, cos_ref, sin_ref,
                       o_ref, kout_ref, k2_ref, v2_ref, *, nc):
    c = pl.program_id(1)

    @pl.when(c == 0)
    def _():
        k2_ref[...] = jnp.zeros_like(k2_ref)
        v2_ref[...] = jnp.zeros_like(v2_ref)

    cos = cos_ref[...]
    sin_signed = sin_ref[...]
    slot = c % KEY_SLOTS
    row0 = pl.multiple_of(slot * CHUNK, CHUNK)
    for ti in range(ATT_KV_WIDTH // LANES):
        k_rot = _rope(k_ref[:, ti * LANES:(ti + 1) * LANES], cos, sin_signed)
        _store_keys(k2_ref, ti, row0, CHUNK, k_rot)
        _store_keys(v2_ref, ti, row0, CHUNK, v_ref[:, ti * LANES:(ti + 1) * LANES])
        for back in range(WINDOW // CHUNK):
            @pl.when(c == nc - 1 - back)
            def _():
                r0 = WINDOW - (back + 1) * CHUNK
                kout_ref[0, r0:r0 + CHUNK, ti * LANES:(ti + 1) * LANES] = k_rot

    key_slot = (lax.broadcasted_iota(jnp.int32, (CHUNK, PAIR * KEYS_PAD), 1) % KEYS_PAD) // CHUNK
    valid = (key_slot <= c) & (key_slot != (c + 1) % KEY_SLOTS)
    _attend_pairs(q_ref, z_ref, cos, sin_signed, sinks_ref, k2_ref, v2_ref, o_ref, valid, unroll=4)


def _swa_prompt(main, sinks, cos_t, sin_t, *, n, t):
    nc = t // CHUNK
    kvw = ATT_KV_WIDTH
    grid_spec = pltpu.PrefetchScalarGridSpec(
        num_scalar_prefetch=1,
        grid=(n, nc),
        in_specs=[
            pl.BlockSpec((CHUNK, ATT_WIDTH), lambda b, c, s: (b * nc + c, COL_QAT // ATT_WIDTH)),
            pl.BlockSpec((CHUNK, ATT_WIDTH), lambda b, c, s: (b * nc + c, COL_ZAT // ATT_WIDTH)),
            pl.BlockSpec((CHUNK, kvw), lambda b, c, s: (b * nc + c, COL_KAT // kvw)),
            pl.BlockSpec((CHUNK, kvw), lambda b, c, s: (b * nc + c, COL_VAT // kvw)),
            pl.BlockSpec((CHUNK, LANES), lambda b, c, s: (c, 0)),
            pl.BlockSpec((CHUNK, LANES), lambda b, c, s: (c, 0)),
        ],
        out_specs=[
            pl.BlockSpec((CHUNK, ATT_WIDTH), lambda b, c, s: (b * nc + c, 0)),
            pl.BlockSpec((1, WINDOW, kvw), lambda b, c, s: (b, 0, 0)),
        ],
        scratch_shapes=[
            pltpu.VMEM((ATT_KV_HEADS, PAIR * KEYS_PAD, LANES), BF16),
            pltpu.VMEM((ATT_KV_HEADS, PAIR * KEYS_PAD, LANES), BF16),
        ],
    )
    return pl.pallas_call(
        functools.partial(_swa_prompt_kernel, nc=nc),
        grid_spec=grid_spec,
        out_shape=[
            jax.ShapeDtypeStruct((n * t, ATT_WIDTH), BF16),
            jax.ShapeDtypeStruct((n, WINDOW, kvw), F32),
        ],
        compiler_params=_cparams(("arbitrary", "arbitrary")),
        name="swa_prompt",
    )(sinks, main, main, main, main, cos_t, sin_t)


def _swa_sample_kernel(sinks_ref, q_ref, z_ref, k_ref, v_ref, ck_ref, cv_ref, cos_ref, sin_ref,
                       o_ref, kout_ref, k2_ref, v2_ref, *, t, past):
    k2_ref[...] = jnp.zeros_like(k2_ref)
    v2_ref[...] = jnp.zeros_like(v2_ref)
    cos = cos_ref[...]
    sin_signed = sin_ref[...]
    for ti in range(ATT_KV_WIDTH // LANES):
        cols = slice(ti * LANES, (ti + 1) * LANES)
        k_rot = _rope(k_ref[:, cols], cos, sin_signed)
        _store_keys(k2_ref, ti, 0, past, ck_ref[0, :, cols])
        _store_keys(k2_ref, ti, past, t, k_rot)
        _store_keys(v2_ref, ti, 0, past, cv_ref[0, :, cols])
        _store_keys(v2_ref, ti, past, t, v_ref[:, cols])
        kout_ref[0, 0:past - t, cols] = ck_ref[0, t:past, cols]
        kout_ref[0, past - t:past, cols] = k_rot
    key_pos = lax.broadcasted_iota(jnp.int32, (t, PAIR * KEYS_PAD), 1) % KEYS_PAD
    valid = key_pos < past + t
    _attend_pairs(q_ref, z_ref, cos, sin_signed, sinks_ref, k2_ref, v2_ref, o_ref, valid, unroll=4)


def _swa_sample(main, sinks, cache_k, cache_v, cos_t, sin_t, *, n, t):
    past = cache_k.shape[1]
    assert past + t <= KEYS_PAD and t <= past
    kvw = ATT_KV_WIDTH
    grid_spec = pltpu.PrefetchScalarGridSpec(
        num_scalar_prefetch=1,
        grid=(n,),
        in_specs=[
            pl.BlockSpec((t, ATT_WIDTH), lambda b, s: (b, COL_QAT // ATT_WIDTH)),
            pl.BlockSpec((t, ATT_WIDTH), lambda b, s: (b, COL_ZAT // ATT_WIDTH)),
            pl.BlockSpec((t, kvw), lambda b, s: (b, COL_KAT // kvw)),
            pl.BlockSpec((t, kvw), lambda b, s: (b, COL_VAT // kvw)),
            pl.BlockSpec((1, past, kvw), lambda b, s: (b, 0, 0)),
            pl.BlockSpec((1, past, kvw), lambda b, s: (b, 0, 0)),
            pl.BlockSpec((t, LANES), lambda b, s: (0, 0)),
            pl.BlockSpec((t, LANES), lambda b, s: (0, 0)),
        ],
        out_specs=[
            pl.BlockSpec((t, ATT_WIDTH), lambda b, s: (b, 0)),
            pl.BlockSpec((1, past, kvw), lambda b, s: (b, 0, 0)),
        ],
        scratch_shapes=[
            pltpu.VMEM((ATT_KV_HEADS, PAIR * KEYS_PAD, LANES), BF16),
            pltpu.VMEM((ATT_KV_HEADS, PAIR * KEYS_PAD, LANES), BF16),
        ],
    )
    return pl.pallas_call(
        functools.partial(_swa_sample_kernel, t=t, past=past),
        grid_spec=grid_spec,
        out_shape=[
            jax.ShapeDtypeStruct((n * t, ATT_WIDTH), BF16),
            jax.ShapeDtypeStruct((n, past, kvw), F32),
        ],
        compiler_params=_cparams(("arbitrary",)),
        name="swa_sample",
    )(sinks, main, main, main, main, cache_k, cache_v, cos_t, sin_t)


def _out_proj_kernel(odn_ref, oat_ref, w1_ref, w2_ref, y_ref):
    y_ref[...] = _dot(odn_ref[...], w1_ref[...]) + _dot(oat_ref[...], w2_ref[...])


def _out_proj(o_dn, o_at, w_out_b, *, tm, tn):
    m = o_dn.shape[0]
    d = w_out_b.shape[1]
    return pl.pallas_call(
        _out_proj_kernel,
        grid=(m // tm, d // tn),
        in_specs=[
            pl.BlockSpec((tm, DN_WIDTH), lambda i, j: (i, 0)),
            pl.BlockSpec((tm, ATT_WIDTH), lambda i, j: (i, 0)),
            pl.BlockSpec((DN_WIDTH, tn), lambda i, j: (0, j)),
            pl.BlockSpec((ATT_WIDTH, tn), lambda i, j: (DN_WIDTH // ATT_WIDTH, j)),
        ],
        out_specs=pl.BlockSpec((tm, tn), lambda i, j: (i, j)),
        out_shape=jax.ShapeDtypeStruct((m, d), F32),
        compiler_params=_cparams(("arbitrary", "arbitrary")),
        name="out_proj",
    )(o_dn, o_at, w_out_b, w_out_b)


def _finish_kernel(x_ref, y_ref, mod_ref, g_ref, o_ref, *, nb, tt):
    chunks_per_batch = tt // ROW_CHUNK

    def body(r, carry):
        b = r // chunks_per_batch
        r0 = pl.multiple_of((r % chunks_per_batch) * ROW_CHUNK, ROW_CHUNK)
        y = y_ref[pl.ds(pl.multiple_of(r * ROW_CHUNK, ROW_CHUNK), ROW_CHUNK), :]
        yn = y * lax.rsqrt(jnp.mean(y * y, axis=-1, keepdims=True) + NORM_EPS) * g_ref[...]
        gate = mod_ref[b][2:3, :]
        o_ref[b, pl.ds(r0, ROW_CHUNK), :] = x_ref[b, pl.ds(r0, ROW_CHUNK), :] + gate * yn
        return carry

    lax.fori_loop(0, nb * chunks_per_batch, body, 0)


def _finish(x, y, mod, g_post, *, nb, tt):
    n, t, d = x.shape
    tiles_per_batch = t // tt
    return pl.pallas_call(
        functools.partial(_finish_kernel, nb=nb, tt=tt),
        grid=((n // nb) * tiles_per_batch,),
        in_specs=[
            pl.BlockSpec((nb, tt, d), lambda i: (i // tiles_per_batch, i % tiles_per_batch, 0)),
            pl.BlockSpec((nb * tt, d), lambda i: (i, 0)),
            pl.BlockSpec((nb, 3, d), lambda i: (i // tiles_per_batch, 0, 0)),
            pl.BlockSpec((1, d), lambda i: (0, 0)),
        ],
        out_specs=pl.BlockSpec((nb, tt, d), lambda i: (i // tiles_per_batch, i % tiles_per_batch, 0)),
        out_shape=jax.ShapeDtypeStruct((n, t, d), F32),
        compiler_params=_cparams(("arbitrary",)),
        name="finish",
    )(x, y, mod, g_post.reshape(1, d))


def _rope_tables(pos):
    half = ATT_HEAD_DIM // 2
    inv = ROPE_THETA ** (-jnp.arange(half, dtype=F32) / half)
    ang = pos.astype(F32)[:, None] * inv[None, :]
    cos = jnp.cos(ang)
    sin = jnp.sin(ang)
    reps = LANES // ATT_HEAD_DIM
    return jnp.tile(jnp.concatenate([cos, cos], axis=-1), (1, reps)), \
        jnp.tile(jnp.concatenate([-sin, sin], axis=-1), (1, reps))


def _split_w_in(w_in):
    sizes = (DN_CONV_DIM, DN_WIDTH, DN_HEADS, DN_HEADS, ATT_WIDTH, ATT_KV_WIDTH, ATT_KV_WIDTH, ATT_WIDTH)
    offs = [0]
    for s in sizes:
        offs.append(offs[-1] + s)
    part = [w_in[..., offs[i]:offs[i + 1]] for i in range(len(sizes))]
    qkv, z_dn, b_dn, a_dn, q_at, k_at, v_at, z_at = part
    w_main = jnp.concatenate([qkv, z_dn, q_at, z_at, k_at, v_at], axis=-1).astype(BF16)
    pad = jnp.zeros(w_in.shape[:-1] + (BA_DIM - 2 * DN_HEADS,), w_in.dtype)
    w_ba = jnp.concatenate([b_dn, a_dn, pad], axis=-1).astype(BF16)
    return w_main, w_ba


def _layer(x, mod, conv_buf, s0, cache_k, cache_v, pos, lw, *, prompt):
    (g_pre, g_post, w_main, w_ba, conv_w, gpar, dn_norm, sinks, w_out_b) = lw
    n, t, d = x.shape
    if prompt:
        nb, tt, tt_fin, tn, cs = 1, 512, 256, 1280, CHUNK
    else:
        nb, tt, tt_fin, tn, cs = n, t, t, 1280, min(CHUNK, t)
    main, ba = _in_proj(x, mod, g_pre, w_main, w_ba, nb=nb, tt=tt, tn=tn)

    conv_buf8 = jnp.pad(conv_buf, ((0, 0), (SUBLANES - (CONV_W - 1), 0), (0, 0)))
    o_dn, s_new = _deltanet(main, ba, conv_buf8, conv_w, gpar, dn_norm, s0, n=n, t=t, cs=cs)
    main3 = main.reshape(n, t, MAIN_DIM)
    conv_new = main3[:, t - (CONV_W - 1):, COL_QKV:COL_QKV + DN_CONV_DIM]

    cos_t, sin_t = _rope_tables(pos)
    v_rows = main3[:, :, COL_VAT:COL_VAT + ATT_KV_WIDTH]
    if prompt:
        o_at, k_new = _swa_prompt(main, sinks, cos_t, sin_t, n=n, t=t)
        v_new = v_rows[:, t - WINDOW:]
    else:
        past = cache_k.shape[1]
        ck = cache_k.reshape(n, past, ATT_KV_WIDTH)
        cv = cache_v.reshape(n, past, ATT_KV_WIDTH)
        o_at, k_new = _swa_sample(main, sinks, ck, cv, cos_t, sin_t, n=n, t=t)
        v_new = jnp.concatenate([cv, v_rows], axis=1)[:, -past:]
    rows = k_new.shape[1]
    k_new = k_new.reshape(n, rows, ATT_KV_HEADS, ATT_HEAD_DIM)
    v_new = v_new.reshape(n, rows, ATT_KV_HEADS, ATT_HEAD_DIM)

    tm = min(n * t, 1024)
    y = _out_proj(o_dn, o_at, w_out_b, tm=tm, tn=512)
    x_new = _finish(x, y, mod, g_post, nb=nb, tt=tt_fin)
    return x_new, conv_new, s_new, k_new, v_new


def kernel(x_prompt, x_sample, state_conv, state_dn, cache_k, cache_v, c_prompt, c_sample,
           w_ada, b_ada, g_pre, g_post, w_in, conv_w, a_log, dt_bias, dn_norm, sinks, w_out):
    depth = w_in.shape[0]
    bp, tp, d = x_prompt.shape
    bs, ts, _ = x_sample.shape
    pos_p = jnp.arange(tp)
    pos_s = PAST_LEN + jnp.arange(ts)

    w_main, w_ba = _split_w_in(w_in)
    w_out_b = w_out.astype(BF16)
    pad_rows = (-(bp + bs)) % (2 * SUBLANES)
    c_all = jnp.concatenate([c_prompt, c_sample, jnp.zeros((pad_rows, d), F32)], axis=0)
    gate_pad = jnp.zeros((depth, BA_DIM - 2 * DN_HEADS), F32)
    head_pad = jnp.zeros((depth, DN_HEADS), F32)
    gpar = jnp.stack([jnp.concatenate([head_pad, a_log, gate_pad], axis=-1),
                      jnp.concatenate([head_pad, dt_bias, gate_pad], axis=-1)], axis=1)

    zero_conv = jnp.zeros((bp, CONV_W - 1, DN_CONV_DIM), F32)
    zero_state = jnp.zeros((bp, DN_HEADS, DN_HEAD_DIM, DN_HEAD_DIM), F32)

    xp, xs = x_prompt, x_sample
    outs_p, outs_s = [], []
    for l in range(depth):
        ada = _ada(c_all, w_ada[l], b_ada[l])
        mod_p = ada[:bp].reshape(bp, 3, d)
        mod_s = ada[bp:bp + bs].reshape(bs, 3, d)
        lw = (g_pre[l], g_post[l], w_main[l], w_ba[l], conv_w[l], gpar[l], dn_norm[l], sinks[l], w_out_b[l])
        xp, *rest_p = _layer(xp, mod_p, zero_conv, zero_state, None, None, pos_p, lw, prompt=True)
        xs, *rest_s = _layer(xs, mod_s, state_conv[l], state_dn[l], cache_k[l], cache_v[l], pos_s, lw,
                             prompt=False)
        outs_p.append(rest_p)
        outs_s.append(rest_s)

    def stack(outs, i):
        return jnp.stack([o[i] for o in outs])

    return (xp, xs,
            stack(outs_p, 0), stack(outs_p, 1), stack(outs_p, 2), stack(outs_p, 3),
            stack(outs_s, 0), stack(outs_s, 1), stack(outs_s, 2), stack(outs_s, 3))
```

```python
import functools
import math

import jax
import jax.numpy as jnp
from jax import lax
from jax.experimental import pallas as pl
from jax.experimental.pallas import tpu as pltpu

F32 = jnp.float32
BF16 = jnp.bfloat16

D_MODEL = 4096
CHUNK = 64
DN_WIDTH = D_MODEL // 2
DN_HEAD_DIM = 128
DN_HEADS = DN_WIDTH // DN_HEAD_DIM
CONV_W = 4
DN_CONV_DIM = 3 * DN_WIDTH
ATT_WIDTH = D_MODEL - DN_WIDTH
ATT_HEAD_DIM = 64
ATT_Q_HEADS = ATT_WIDTH // ATT_HEAD_DIM
ATT_KV_HEADS = ATT_Q_HEADS // 8
ATT_KV_WIDTH = ATT_KV_HEADS * ATT_HEAD_DIM
WINDOW = 128
ROPE_THETA = 10000.0
NORM_EPS = 1e-6
PAST_LEN = 2048

LANES = 128
SUBLANES = 8
VMEM_LIMIT_BYTES = 56 * 1024 * 1024

COL_QKV = 0
COL_ZDN = COL_QKV + DN_CONV_DIM
COL_QAT = COL_ZDN + DN_WIDTH
COL_ZAT = COL_QAT + ATT_WIDTH
COL_KAT = COL_ZAT + ATT_WIDTH
COL_VAT = COL_KAT + ATT_KV_WIDTH
MAIN_DIM = COL_VAT + ATT_KV_WIDTH
BA_DIM = LANES

ROW_CHUNK = 8
ROW_GROUP = 8
PAIR = LANES // ATT_HEAD_DIM
N_PAIRS = ATT_Q_HEADS // PAIR
KEY_SLOTS = 4
KEYS_PAD = KEY_SLOTS * CHUNK


def _cparams(sem):
    return pltpu.CompilerParams(dimension_semantics=sem, vmem_limit_bytes=VMEM_LIMIT_BYTES)


def _dot(a, b):
    return jnp.dot(a, b, preferred_element_type=F32)


def _dot_nt(a, b):
    return lax.dot_general(a, b, (((1,), (1,)), ((), ())), preferred_element_type=F32)


def _dot_tn(a, b):
    return lax.dot_general(a, b, (((0,), (0,)), ((), ())), preferred_element_type=F32)


def _split(a):
    hi = a.astype(BF16)
    lo = (a - hi.astype(F32)).astype(BF16)
    return hi, lo


def _dot_split(a, b):
    ah, al = _split(a)
    bh, bl = _split(b)
    return _dot(ah, bh) + (_dot(ah, bl) + _dot(al, bh))


def _sigmoid(x):
    return 1.0 / (1.0 + jnp.exp(-x))


def _silu(x):
    return x * _sigmoid(x)


def _ada_kernel(c_ref, w_ref, b_ref, o_ref):
    s = _silu(c_ref[...]).astype(BF16)
    o_ref[...] = _dot(s, w_ref[...].astype(BF16)) + b_ref[...]


def _ada(c_all, w_ada, b_ada):
    rows, d = c_all.shape
    depth, _, n_out = w_ada.shape
    tn = 512
    return pl.pallas_call(
        _ada_kernel,
        grid=(depth, n_out // tn),
        in_specs=[
            pl.BlockSpec((rows, d), lambda l, j: (0, 0)),
            pl.BlockSpec((None, d, tn), lambda l, j: (l, 0, j)),
            pl.BlockSpec((None, 1, tn), lambda l, j: (l, 0, j)),
        ],
        out_specs=pl.BlockSpec((None, rows, tn), lambda l, j: (l, 0, j)),
        out_shape=jax.ShapeDtypeStruct((depth, rows, n_out), F32),
        compiler_params=_cparams(("arbitrary", "arbitrary")),
        name="ada",
    )(c_all, w_ada, b_ada.reshape(depth, 1, n_out))


def _rows_bcast(v):
    return jnp.broadcast_to(v[..., None, :], v.shape[:-1] + (ROW_CHUNK, v.shape[-1]))


def _row_chunks(n_rows, rows_per_batch, body):
    chunks_per_batch = rows_per_batch // ROW_CHUNK
    group = math.gcd(ROW_GROUP, n_rows // ROW_CHUNK)

    def step(r, carry):
        chunks = []
        for u in range(group):
            ci = r * group + u
            chunks.append((pl.ds(pl.multiple_of(ci * ROW_CHUNK, ROW_CHUNK), ROW_CHUNK), ci // chunks_per_batch))
        body(chunks)
        return carry

    lax.fori_loop(0, n_rows // (ROW_CHUNK * group), step, 0)


def _inv_rms(xs):
    return [lax.rsqrt(jnp.mean(x * x, axis=-1, keepdims=True) + NORM_EPS) for x in xs]


def _prenorm_kernel(x_ref, mod_ref, g_ref, h_ref, *, tt):
    def body(chunks):
        xs = [x_ref[rows, :] for rows, _ in chunks]
        for (rows, b), x, r in zip(chunks, xs, _inv_rms(xs)):
            h_ref[rows, :] = (x * r * g_ref[...] * (1.0 + mod_ref[b, 1]) + mod_ref[b, 0]).astype(BF16)

    _row_chunks(x_ref.shape[0], tt, body)


def _prenorm(x2, mod, g_pre, *, nb, tt):
    m, d = x2.shape
    tm = nb * tt
    tiles_per_batch = (m // mod.shape[0]) // tt
    return pl.pallas_call(
        functools.partial(_prenorm_kernel, tt=tt),
        grid=(m // tm,),
        in_specs=[
            pl.BlockSpec((tm, d), lambda i: (i, 0)),
            pl.BlockSpec((nb, 3, ROW_CHUNK, d), lambda i: (i // tiles_per_batch, 0, 0, 0)),
            pl.BlockSpec((ROW_CHUNK, d), lambda i: (0, 0)),
        ],
        out_specs=pl.BlockSpec((tm, d), lambda i: (i, 0)),
        out_shape=jax.ShapeDtypeStruct((m, d), BF16),
        compiler_params=_cparams(("arbitrary",)),
        name="prenorm",
    )(x2, mod, _rows_bcast(g_pre))


def _in_proj_kernel(h_ref, w_ref, wba_ref, o_ref, oba_ref):
    @pl.when(pl.program_id(1) == 0)
    def _():
        oba_ref[...] = _dot(h_ref[...], wba_ref[...])

    o_ref[...] = _dot(h_ref[...], w_ref[...])


def _in_proj(h, w_main, w_ba, layer, *, tm, tn):
    m, d = h.shape
    return pl.pallas_call(
        _in_proj_kernel,
        grid=(m // tm, MAIN_DIM // tn),
        in_specs=[
            pl.BlockSpec((tm, d), lambda i, j: (i, 0)),
            pl.BlockSpec((None, d, tn), lambda i, j: (layer, 0, j)),
            pl.BlockSpec((None, d, BA_DIM), lambda i, j: (layer, 0, 0)),
        ],
        out_specs=[
            pl.BlockSpec((tm, tn), lambda i, j: (i, j)),
            pl.BlockSpec((tm, BA_DIM), lambda i, j: (i, 0)),
        ],
        out_shape=[
            jax.ShapeDtypeStruct((m, MAIN_DIM), F32),
            jax.ShapeDtypeStruct((m, BA_DIM), F32),
        ],
        compiler_params=_cparams(("arbitrary", "arbitrary")),
        name="in_proj",
    )(h, w_main, w_ba)


def _deltanet_kernel(qkv_ref, z_ref, ba_ref, cbuf_ref, cw_ref, gpar_ref, ng_ref, s0_ref,
                     o_ref, s_ref, ext_ref, gct_ref, *, cs):
    c = pl.program_id(1)
    hd = DN_HEAD_DIM

    @pl.when(c == 0)
    def _():
        s_ref[...] = s0_ref[...]
        ext_ref[0:SUBLANES, :] = cbuf_ref[0]

    ext_ref[SUBLANES:SUBLANES + cs, :] = qkv_ref[...]

    ba = ba_ref[...]
    beta_all = _sigmoid(ba)
    xg = ba + gpar_ref[1:2, :]
    softplus = jnp.maximum(xg, 0.0) + jnp.log(1.0 + jnp.exp(-jnp.abs(xg)))
    g_all = -jnp.exp(gpar_ref[0:1, :]) * softplus
    row = lax.broadcasted_iota(jnp.int32, (cs, LANES), 0)
    gc_all = g_all
    shift = 1
    while shift < cs:
        gc_all = gc_all + jnp.where(row >= shift, pltpu.roll(gc_all, shift, axis=0), 0.0)
        shift *= 2
    gct_ref[...] = gc_all.T
    lane = lax.broadcasted_iota(jnp.int32, (cs, LANES), 1)

    ri = lax.broadcasted_iota(jnp.int32, (cs, cs), 0)
    ci = lax.broadcasted_iota(jnp.int32, (cs, cs), 1)
    causal = ri >= ci
    strict = ri > ci
    eye = jnp.where(ri == ci, 1.0, 0.0).astype(F32)
    n_doublings = int(math.log2(cs)) - 1

    def conv_silu(col0):
        cols = slice(col0, col0 + hd)
        first = SUBLANES - (CONV_W - 1)
        acc = ext_ref[first:first + cs, cols] * cw_ref[0:1, cols]
        for j in range(1, CONV_W):
            acc = acc + ext_ref[first + j:first + j + cs, cols] * cw_ref[j:j + 1, cols]
        return _silu(acc)

    heads = range(DN_HEADS)
    q, k, v = [], [], []
    for h in heads:
        qh = conv_silu(h * hd)
        kh = conv_silu(DN_WIDTH + h * hd)
        q.append(qh * lax.rsqrt(jnp.sum(qh * qh, axis=-1, keepdims=True) + NORM_EPS) * hd ** -0.5)
        k.append(kh * lax.rsqrt(jnp.sum(kh * kh, axis=-1, keepdims=True) + NORM_EPS))
        v.append(conv_silu(2 * DN_WIDTH + h * hd))

    beta = [jnp.sum(jnp.where(lane == h, beta_all, 0.0), axis=-1, keepdims=True) for h in heads]
    gcol = [jnp.sum(jnp.where(lane == DN_HEADS + h, gc_all, 0.0), axis=-1, keepdims=True) for h in heads]
    grow = [gct_ref[DN_HEADS + h:DN_HEADS + h + 1, :] for h in heads]
    glast = [g[:, cs - 1:cs] for g in grow]
    gamma = [jnp.where(causal, jnp.exp(gcol[h] - grow[h]), 0.0) for h in heads]
    ecol = [jnp.exp(g) for g in gcol]

    kb = [x.astype(BF16) for x in k]
    kq = [_dot_nt(jnp.concatenate([kb[h], q[h].astype(BF16)], axis=0), kb[h]) for h in heads]
    a_mat = [jnp.where(strict, beta[h] * kq[h][:cs] * gamma[h], 0.0) for h in heads]
    qk = [kq[h][cs:] * gamma[h] for h in heads]

    inv = [eye - a for a in a_mat]
    apow = [a.astype(BF16) for a in a_mat]
    for _ in range(n_doublings):
        apow = [_dot(a, a).astype(BF16) for a in apow]
        inv = [inv[h] + _dot(inv[h].astype(BF16), apow[h]) for h in heads]
    resid = [eye - inv[h] - _dot_split(a_mat[h], inv[h]) for h in heads]
    inv = [inv[h] + _dot(inv[h].astype(BF16), resid[h].astype(BF16)) for h in heads]

    rhs = [jnp.concatenate([beta[h] * v[h], beta[h] * ecol[h] * k[h]], axis=-1).astype(BF16) for h in heads]
    uw = [_dot(inv[h].astype(BF16), rhs[h]) for h in heads]

    state = [s_ref[0, h] for h in heads]
    ws = [_dot(jnp.concatenate([uw[h][:, hd:], q[h] * ecol[h]], axis=0).astype(BF16), state[h].astype(BF16))
          for h in heads]
    v_new = [(uw[h][:, :hd] - ws[h][:cs]).astype(BF16) for h in heads]
    o = [ws[h][cs:] + _dot(qk[h].astype(BF16), v_new[h]) for h in heads]
    for h in heads:
        kdec = k[h] * jnp.exp(glast[h] - gcol[h])
        s_ref[0, h] = state[h] * jnp.exp(glast[h]) + _dot_tn(kdec.astype(BF16), v_new[h])
    for h in heads:
        oh = o[h] * lax.rsqrt(jnp.mean(o[h] * o[h], axis=-1, keepdims=True) + NORM_EPS) * ng_ref[...]
        oh = oh * _silu(z_ref[:, h * hd:(h + 1) * hd])
        o_ref[:, h * hd:(h + 1) * hd] = oh.astype(o_ref.dtype)
    ext_ref[0:SUBLANES, :] = ext_ref[cs:cs + SUBLANES, :]


def _deltanet(main, ba, conv_buf8, conv_w, gpar, dn_norm, s0, *, n, t, cs):
    nc = t // cs
    hd = DN_HEAD_DIM
    return pl.pallas_call(
        functools.partial(_deltanet_kernel, cs=cs),
        grid=(n, nc),
        in_specs=[
            pl.BlockSpec((cs, DN_CONV_DIM), lambda b, c: (b * nc + c, COL_QKV // DN_CONV_DIM)),
            pl.BlockSpec((cs, DN_WIDTH), lambda b, c: (b * nc + c, COL_ZDN // DN_WIDTH)),
            pl.BlockSpec((cs, BA_DIM), lambda b, c: (b * nc + c, 0)),
            pl.BlockSpec((1, SUBLANES, DN_CONV_DIM), lambda b, c: (b, 0, 0)),
            pl.BlockSpec((CONV_W, DN_CONV_DIM), lambda b, c: (0, 0)),
            pl.BlockSpec((2, BA_DIM), lambda b, c: (0, 0)),
            pl.BlockSpec((1, hd), lambda b, c: (0, 0)),
            pl.BlockSpec((1, DN_HEADS, hd, hd), lambda b, c: (b, 0, 0, 0)),
        ],
        out_specs=[
            pl.BlockSpec((cs, DN_WIDTH), lambda b, c: (b * nc + c, 0)),
            pl.BlockSpec((1, DN_HEADS, hd, hd), lambda b, c: (b, 0, 0, 0)),
        ],
        out_shape=[
            jax.ShapeDtypeStruct((n * t, DN_WIDTH), BF16),
            jax.ShapeDtypeStruct((n, DN_HEADS, hd, hd), F32),
        ],
        scratch_shapes=[
            pltpu.VMEM((SUBLANES + cs, DN_CONV_DIM), F32),
            pltpu.VMEM((LANES, cs), F32),
        ],
        compiler_params=_cparams(("arbitrary", "arbitrary")),
        name="deltanet",
    )(main, main, ba, conv_buf8, conv_w, gpar, dn_norm.reshape(1, hd), s0)


def _rope(x, cos, sin_signed):
    lane = lax.broadcasted_iota(jnp.int32, x.shape, 1)
    half = ATT_HEAD_DIM // 2
    swapped = jnp.where((lane % ATT_HEAD_DIM) < half,
                        pltpu.roll(x, LANES - half, axis=1), pltpu.roll(x, half, axis=1))
    return x * cos + swapped * sin_signed


def _split_heads(tile):
    lane = lax.broadcasted_iota(jnp.int32, tile.shape, 1)
    lo = lane < ATT_HEAD_DIM
    rolled = pltpu.roll(tile, ATT_HEAD_DIM, axis=1)
    zero = jnp.zeros_like(tile)
    return (jnp.where(lo, tile, zero), jnp.where(lo, zero, rolled),
            jnp.where(lo, rolled, zero), jnp.where(lo, zero, tile))


def _store_keys(dst_ref, tile_idx, row0, rows, tile):
    a_top, a_bot, b_top, b_bot = _split_heads(tile)
    ja = PAIR * tile_idx
    dst_ref[ja, pl.ds(row0, rows), :] = a_top.astype(BF16)
    dst_ref[ja, pl.ds(KEYS_PAD + row0, rows), :] = a_bot.astype(BF16)
    dst_ref[ja + 1, pl.ds(row0, rows), :] = b_top.astype(BF16)
    dst_ref[ja + 1, pl.ds(KEYS_PAD + row0, rows), :] = b_bot.astype(BF16)


def _attend_pairs(q_ref, z_ref, cos, sin_signed, sinks_ref, k2_ref, v2_ref, o_ref, valid, group=8):
    rows = q_ref.shape[0]
    lane = lax.broadcasted_iota(jnp.int32, (rows, LANES), 1)
    lo = lane < ATT_HEAD_DIM
    neg_inf = jnp.full((rows, PAIR * KEYS_PAD), -jnp.inf, F32)

    def kv_of(p):
        return p // (N_PAIRS // ATT_KV_HEADS)

    def scores(p):
        cols = slice(p * LANES, (p + 1) * LANES)
        q = _rope(q_ref[:, cols], cos, sin_signed) * ATT_HEAD_DIM ** -0.5
        return _dot_nt(q.astype(BF16), k2_ref[kv_of(p)])

    def softmax(p, s):
        s = jnp.where(valid, s, neg_inf)
        probs = []
        inv_den = []
        for hh in range(PAIR):
            sh = s[:, hh * KEYS_PAD:(hh + 1) * KEYS_PAD]
            sink = sinks_ref[PAIR * p + hh]
            m = jnp.maximum(jnp.max(sh, axis=-1, keepdims=True), sink)
            ph = jnp.exp(sh - m)
            den = jnp.sum(ph, axis=-1, keepdims=True) + jnp.exp(sink - m)
            probs.append(ph)
            inv_den.append(1.0 / den)
        return jnp.concatenate(probs, axis=-1).astype(BF16), jnp.where(lo, inv_den[0], inv_den[1])

    def finish(p, probs, inv_den):
        cols = slice(p * LANES, (p + 1) * LANES)
        o = _dot(probs, v2_ref[kv_of(p)]) * inv_den
        o_ref[:, cols] = (o * _silu(z_ref[:, cols])).astype(o_ref.dtype)

    groups = [range(g, g + group) for g in range(0, N_PAIRS, group)]
    s_next = [scores(p) for p in groups[0]]
    for gi, pairs in enumerate(groups):
        s_cur = s_next
        if gi + 1 < len(groups):
            s_next = [scores(p) for p in groups[gi + 1]]
        soft = [softmax(p, s) for p, s in zip(pairs, s_cur)]
        for p, (probs, inv_den) in zip(pairs, soft):
            finish(p, probs, inv_den)


def _swa_prompt_kernel(sinks_ref, q_ref, z_ref, k_ref, v_ref, cos_ref, sin_ref,
                       o_ref, kout_ref, k2_ref, v2_ref, *, nc):
    c = pl.program_id(1)

    @pl.when(c == 0)
    def _():
        k2_ref[...] = jnp.zeros_like(k2_ref)
        v2_ref[...] = jnp.zeros_like(v2_ref)

    cos = cos_ref[...]
    sin_signed = sin_ref[...]
    slot = c % KEY_SLOTS
    row0 = pl.multiple_of(slot * CHUNK, CHUNK)
    for ti in range(ATT_KV_WIDTH // LANES):
        k_rot = _rope(k_ref[:, ti * LANES:(ti + 1) * LANES], cos, sin_signed)
        _store_keys(k2_ref, ti, row0, CHUNK, k_rot)
        _store_keys(v2_ref, ti, row0, CHUNK, v_ref[:, ti * LANES:(ti + 1) * LANES])
        for back in range(WINDOW // CHUNK):
            @pl.when(c == nc - 1 - back)
            def _():
                r0 = WINDOW - (back + 1) * CHUNK
                kout_ref[0, r0:r0 + CHUNK, ti * LANES:(ti + 1) * LANES] = k_rot

    key_slot = (lax.broadcasted_iota(jnp.int32, (CHUNK, PAIR * KEYS_PAD), 1) % KEYS_PAD) // CHUNK
    valid = (key_slot <= c) & (key_slot != (c + 1) % KEY_SLOTS)
    _attend_pairs(q_ref, z_ref, cos, sin_signed, sinks_ref, k2_ref, v2_ref, o_ref, valid)


def _swa_prompt(main, sinks, cos_t, sin_t, *, n, t):
    nc = t // CHUNK
    kvw = ATT_KV_WIDTH
    grid_spec = pltpu.PrefetchScalarGridSpec(
        num_scalar_prefetch=1,
        grid=(n, nc),
        in_specs=[
            pl.BlockSpec((CHUNK, ATT_WIDTH), lambda b, c, s: (b * nc + c, COL_QAT // ATT_WIDTH)),
            pl.BlockSpec((CHUNK, ATT_WIDTH), lambda b, c, s: (b * nc + c, COL_ZAT // ATT_WIDTH)),
            pl.BlockSpec((CHUNK, kvw), lambda b, c, s: (b * nc + c, COL_KAT // kvw)),
            pl.BlockSpec((CHUNK, kvw), lambda b, c, s: (b * nc + c, COL_VAT // kvw)),
            pl.BlockSpec((CHUNK, LANES), lambda b, c, s: (c, 0)),
            pl.BlockSpec((CHUNK, LANES), lambda b, c, s: (c, 0)),
        ],
        out_specs=[
            pl.BlockSpec((CHUNK, ATT_WIDTH), lambda b, c, s: (b * nc + c, 0)),
            pl.BlockSpec((1, WINDOW, kvw), lambda b, c, s: (b, 0, 0)),
        ],
        scratch_shapes=[
            pltpu.VMEM((ATT_KV_HEADS, PAIR * KEYS_PAD, LANES), BF16),
            pltpu.VMEM((ATT_KV_HEADS, PAIR * KEYS_PAD, LANES), BF16),
        ],
    )
    return pl.pallas_call(
        functools.partial(_swa_prompt_kernel, nc=nc),
        grid_spec=grid_spec,
        out_shape=[
            jax.ShapeDtypeStruct((n * t, ATT_WIDTH), BF16),
            jax.ShapeDtypeStruct((n, WINDOW, kvw), F32),
        ],
        compiler_params=_cparams(("arbitrary", "arbitrary")),
        name="swa_prompt",
    )(sinks, main, main, main, main, cos_t, sin_t)


def _swa_sample_kernel(sinks_ref, q_ref, z_ref, k_ref, v_ref, ck_ref, cv_ref, cos_ref, sin_ref,
                       o_ref, kout_ref, k2_ref, v2_ref, *, t, past):
    k2_ref[...] = jnp.zeros_like(k2_ref)
    v2_ref[...] = jnp.zeros_like(v2_ref)
    cos = cos_ref[...]
    sin_signed = sin_ref[...]
    for ti in range(ATT_KV_WIDTH // LANES):
        cols = slice(ti * LANES, (ti + 1) * LANES)
        k_rot = _rope(k_ref[:, cols], cos, sin_signed)
        _store_keys(k2_ref, ti, 0, past, ck_ref[0, :, cols])
        _store_keys(k2_ref, ti, past, t, k_rot)
        _store_keys(v2_ref, ti, 0, past, cv_ref[0, :, cols])
        _store_keys(v2_ref, ti, past, t, v_ref[:, cols])
        kout_ref[0, 0:past - t, cols] = ck_ref[0, t:past, cols]
        kout_ref[0, past - t:past, cols] = k_rot
    key_pos = lax.broadcasted_iota(jnp.int32, (t, PAIR * KEYS_PAD), 1) % KEYS_PAD
    valid = key_pos < past + t
    _attend_pairs(q_ref, z_ref, cos, sin_signed, sinks_ref, k2_ref, v2_ref, o_ref, valid)


def _swa_sample(main, sinks, cache_k, cache_v, cos_t, sin_t, *, n, t):
    past = cache_k.shape[1]
    assert past + t <= KEYS_PAD and t <= past
    kvw = ATT_KV_WIDTH
    grid_spec = pltpu.PrefetchScalarGridSpec(
        num_scalar_prefetch=1,
        grid=(n,),
        in_specs=[
            pl.BlockSpec((t, ATT_WIDTH), lambda b, s: (b, COL_QAT // ATT_WIDTH)),
            pl.BlockSpec((t, ATT_WIDTH), lambda b, s: (b, COL_ZAT // ATT_WIDTH)),
            pl.BlockSpec((t, kvw), lambda b, s: (b, COL_KAT // kvw)),
            pl.BlockSpec((t, kvw), lambda b, s: (b, COL_VAT // kvw)),
            pl.BlockSpec((1, past, kvw), lambda b, s: (b, 0, 0)),
            pl.BlockSpec((1, past, kvw), lambda b, s: (b, 0, 0)),
            pl.BlockSpec((t, LANES), lambda b, s: (0, 0)),
            pl.BlockSpec((t, LANES), lambda b, s: (0, 0)),
        ],
        out_specs=[
            pl.BlockSpec((t, ATT_WIDTH), lambda b, s: (b, 0)),
            pl.BlockSpec((1, past, kvw), lambda b, s: (b, 0, 0)),
        ],
        scratch_shapes=[
            pltpu.VMEM((ATT_KV_HEADS, PAIR * KEYS_PAD, LANES), BF16),
            pltpu.VMEM((ATT_KV_HEADS, PAIR * KEYS_PAD, LANES), BF16),
        ],
    )
    return pl.pallas_call(
        functools.partial(_swa_sample_kernel, t=t, past=past),
        grid_spec=grid_spec,
        out_shape=[
            jax.ShapeDtypeStruct((n * t, ATT_WIDTH), BF16),
            jax.ShapeDtypeStruct((n, past, kvw), F32),
        ],
        compiler_params=_cparams(("arbitrary",)),
        name="swa_sample",
    )(sinks, main, main, main, main, cache_k, cache_v, cos_t, sin_t)


def _out_proj_kernel(odn_ref, oat_ref, w_ref, x_ref, mod_ref, gpost_ref, *rest, tt, nk, emit_h):
    if emit_h:
        modn_ref, gpre_ref, xo_ref, h_ref = rest
    else:
        (xo_ref,) = rest
    kk = pl.program_id(1)
    half = nk // 2

    @pl.when(kk == 0)
    def _():
        xo_ref[...] = jnp.zeros_like(xo_ref)

    @pl.when(kk < half)
    def _():
        xo_ref[...] += _dot(odn_ref[...], w_ref[...])

    @pl.when(kk >= half)
    def _():
        xo_ref[...] += _dot(oat_ref[...], w_ref[...])

    @pl.when(kk == nk - 1)
    def _():
        def body(chunks):
            ys = [xo_ref[rows, :] for rows, _ in chunks]
            x_new = [x_ref[rows, :] + mod_ref[b, 2] * (y * r * gpost_ref[...])
                     for (rows, b), y, r in zip(chunks, ys, _inv_rms(ys))]
            for (rows, _), xn in zip(chunks, x_new):
                xo_ref[rows, :] = xn
            if emit_h:
                for (rows, b), xn, r in zip(chunks, x_new, _inv_rms(x_new)):
                    h_ref[rows, :] = (xn * r * gpre_ref[...] * (1.0 + modn_ref[b, 1])
                                      + modn_ref[b, 0]).astype(BF16)

        _row_chunks(xo_ref.shape[0], tt, body)


def _out_proj(o_dn, o_at, w_out_b, layer, x2, mod, g_post, mod_next, g_pre_next, *, nb, tt, tk):
    m, d = x2.shape
    tm = nb * tt
    nk = (DN_WIDTH + ATT_WIDTH) // tk
    half = nk // 2
    tiles_per_batch = (m // mod.shape[0]) // tt
    emit_h = mod_next is not None
    mod_spec = pl.BlockSpec((nb, 3, ROW_CHUNK, d), lambda i, k: (i // tiles_per_batch, 0, 0, 0))
    vec_spec = pl.BlockSpec((ROW_CHUNK, d), lambda i, k: (0, 0))
    row_spec = pl.BlockSpec((tm, d), lambda i, k: (i, 0))
    in_specs = [
        pl.BlockSpec((tm, tk), lambda i, k: (i, jnp.minimum(k, half - 1))),
        pl.BlockSpec((tm, tk), lambda i, k: (i, jnp.maximum(k - half, 0))),
        pl.BlockSpec((None, tk, d), lambda i, k: (layer, k, 0)),
        row_spec, mod_spec, vec_spec,
    ]
    args = [o_dn, o_at, w_out_b, x2, mod, _rows_bcast(g_post)]
    out_specs = [row_spec]
    out_shape = [jax.ShapeDtypeStruct((m, d), F32)]
    if emit_h:
        in_specs += [mod_spec, vec_spec]
        args += [mod_next, _rows_bcast(g_pre_next)]
        out_specs.append(row_spec)
        out_shape.append(jax.ShapeDtypeStruct((m, d), BF16))
    outs = pl.pallas_call(
        functools.partial(_out_proj_kernel, tt=tt, nk=nk, emit_h=emit_h),
        grid=(m // tm, nk),
        in_specs=in_specs,
        out_specs=out_specs,
        out_shape=out_shape,
        compiler_params=_cparams(("arbitrary", "arbitrary")),
        name="out_proj",
    )(*args)
    return (outs[0], outs[1]) if emit_h else (outs[0], None)


def _rope_tables(pos):
    half = ATT_HEAD_DIM // 2
    inv = ROPE_THETA ** (-jnp.arange(half, dtype=F32) / half)
    ang = pos.astype(F32)[:, None] * inv[None, :]
    cos = jnp.cos(ang)
    sin = jnp.sin(ang)
    reps = LANES // ATT_HEAD_DIM
    return jnp.tile(jnp.concatenate([cos, cos], axis=-1), (1, reps)), \
        jnp.tile(jnp.concatenate([-sin, sin], axis=-1), (1, reps))


def _split_w_in(w_in):
    sizes = (DN_CONV_DIM, DN_WIDTH, DN_HEADS, DN_HEADS, ATT_WIDTH, ATT_KV_WIDTH, ATT_KV_WIDTH, ATT_WIDTH)
    offs = [0]
    for s in sizes:
        offs.append(offs[-1] + s)
    part = [w_in[..., offs[i]:offs[i + 1]].astype(BF16) for i in range(len(sizes))]
    qkv, z_dn, b_dn, a_dn, q_at, k_at, v_at, z_at = part
    w_main = jnp.concatenate([qkv, z_dn, q_at, z_at, k_at, v_at], axis=-1)
    pad = jnp.zeros(w_in.shape[:-1] + (BA_DIM - 2 * DN_HEADS,), BF16)
    w_ba = jnp.concatenate([b_dn, a_dn, pad], axis=-1)
    return w_main, w_ba


def _layer(x2, h, mod, mod_next, conv_buf, s0, cache_k, cache_v, pos, lw, layer, *, n, t, prompt):
    (g_post, g_pre_next, w_main, w_ba, conv_w, gpar, dn_norm, sinks, w_out_b) = lw
    if prompt:
        nb, tt, tm_in, cs = 1, 512, min(1024, n * t), CHUNK
    else:
        nb, tt, tm_in, cs = n, t, n * t, min(CHUNK, t)
    main, ba = _in_proj(h, w_main, w_ba, layer, tm=tm_in, tn=1280)

    conv_buf8 = jnp.pad(conv_buf, ((0, 0), (SUBLANES - (CONV_W - 1), 0), (0, 0)))
    o_dn, s_new = _deltanet(main, ba, conv_buf8, conv_w, gpar, dn_norm, s0, n=n, t=t, cs=cs)
    main3 = main.reshape(n, t, MAIN_DIM)
    conv_new = main3[:, t - (CONV_W - 1):, COL_QKV:COL_QKV + DN_CONV_DIM]

    cos_t, sin_t = _rope_tables(pos)
    v_rows = main3[:, :, COL_VAT:COL_VAT + ATT_KV_WIDTH]
    if prompt:
        o_at, k_new = _swa_prompt(main, sinks, cos_t, sin_t, n=n, t=t)
        v_new = v_rows[:, t - WINDOW:]
    else:
        past = cache_k.shape[1]
        ck = cache_k.reshape(n, past, ATT_KV_WIDTH)
        cv = cache_v.reshape(n, past, ATT_KV_WIDTH)
        o_at, k_new = _swa_sample(main, sinks, ck, cv, cos_t, sin_t, n=n, t=t)
        v_new = jnp.concatenate([cv, v_rows], axis=1)[:, -past:]
    rows = k_new.shape[1]
    k_new = k_new.reshape(n, rows, ATT_KV_HEADS, ATT_HEAD_DIM)
    v_new = v_new.reshape(n, rows, ATT_KV_HEADS, ATT_HEAD_DIM)

    x_new, h_next = _out_proj(o_dn, o_at, w_out_b, layer, x2, mod, g_post, mod_next, g_pre_next,
                              nb=nb, tt=tt, tk=512)
    return x_new, h_next, conv_new, s_new, k_new, v_new


def kernel(x_prompt, x_sample, state_conv, state_dn, cache_k, cache_v, c_prompt, c_sample,
           w_ada, b_ada, g_pre, g_post, w_in, conv_w, a_log, dt_bias, dn_norm, sinks, w_out):
    depth = w_in.shape[0]
    bp, tp, d = x_prompt.shape
    bs, ts, _ = x_sample.shape
    pos_p = jnp.arange(tp)
    pos_s = PAST_LEN + jnp.arange(ts)

    w_main, w_ba = _split_w_in(w_in)
    w_out_b = w_out.astype(BF16)
    pad_rows = (-(bp + bs)) % (2 * SUBLANES)
    c_all = jnp.concatenate([c_prompt, c_sample, jnp.zeros((pad_rows, d), F32)], axis=0)
    gate_pad = jnp.zeros((depth, BA_DIM - 2 * DN_HEADS), F32)
    head_pad = jnp.zeros((depth, DN_HEADS), F32)
    gpar = jnp.stack([jnp.concatenate([head_pad, a_log, gate_pad], axis=-1),
                      jnp.concatenate([head_pad, dt_bias, gate_pad], axis=-1)], axis=1)

    zero_conv = jnp.zeros((bp, CONV_W - 1, DN_CONV_DIM), F32)
    zero_state = jnp.zeros((bp, DN_HEADS, DN_HEAD_DIM, DN_HEAD_DIM), F32)

    ada = _ada(c_all, w_ada, b_ada)
    mod_p = _rows_bcast(ada[:, :bp].reshape(depth, bp, 3, d))
    mod_s = _rows_bcast(ada[:, bp:bp + bs].reshape(depth, bs, 3, d))

    xp = x_prompt.reshape(bp * tp, d)
    xs = x_sample.reshape(bs * ts, d)
    hp = _prenorm(xp, mod_p[0], g_pre[0], nb=1, tt=512)
    hs = _prenorm(xs, mod_s[0], g_pre[0], nb=bs, tt=ts)
    outs_p, outs_s = [], []
    for l in range(depth):
        last = l == depth - 1
        lw = (g_post[l], None if last else g_pre[l + 1], w_main, w_ba, conv_w[l], gpar[l], dn_norm[l],
              sinks[l], w_out_b)
        xp, hp, *rest_p = _layer(xp, hp, mod_p[l], None if last else mod_p[l + 1], zero_conv, zero_state,
                                 None, None, pos_p, lw, l, n=bp, t=tp, prompt=True)
        xs, hs, *rest_s = _layer(xs, hs, mod_s[l], None if last else mod_s[l + 1], state_conv[l], state_dn[l],
                                 cache_k[l], cache_v[l], pos_s, lw, l, n=bs, t=ts, prompt=False)
        outs_p.append(rest_p)
        outs_s.append(rest_s)
    xp = xp.reshape(bp, tp, d)
    xs = xs.reshape(bs, ts, d)

    def stack(outs, i):
        return jnp.stack([o[i] for o in outs])

    return (xp, xs,
            stack(outs_p, 0), stack(outs_p, 1), stack(outs_p, 2), stack(outs_p, 3),
            stack(outs_s, 0), stack(outs_s, 1), stack(outs_s, 2), stack(outs_s, 3))
```

```python
import functools
import math

import jax
import jax.numpy as jnp
from jax import lax
from jax.experimental import pallas as pl
from jax.experimental.pallas import tpu as pltpu

F32 = jnp.float32
BF16 = jnp.bfloat16

D_MODEL = 4096
CHUNK = 64
DN_WIDTH = D_MODEL // 2
DN_HEAD_DIM = 128
DN_HEADS = DN_WIDTH // DN_HEAD_DIM
CONV_W = 4
DN_CONV_DIM = 3 * DN_WIDTH
ATT_WIDTH = D_MODEL - DN_WIDTH
ATT_HEAD_DIM = 64
ATT_Q_HEADS = ATT_WIDTH // ATT_HEAD_DIM
ATT_KV_HEADS = ATT_Q_HEADS // 8
ATT_KV_WIDTH = ATT_KV_HEADS * ATT_HEAD_DIM
WINDOW = 128
ROPE_THETA = 10000.0
NORM_EPS = 1e-6
PAST_LEN = 2048

LANES = 128
SUBLANES = 8
VMEM_LIMIT_BYTES = 56 * 1024 * 1024

COL_QKV = 0
COL_ZDN = COL_QKV + DN_CONV_DIM
DN_PART_DIM = COL_ZDN + DN_WIDTH
COL_QAT = 0
COL_ZAT = COL_QAT + ATT_WIDTH
COL_KAT = COL_ZAT + ATT_WIDTH
COL_VAT = COL_KAT + ATT_KV_WIDTH
AT_PART_DIM = COL_VAT + ATT_KV_WIDTH
BA_DIM = LANES
SRC_BETA = DN_PART_DIM
SRC_DECAY = SRC_BETA + DN_HEADS
SRC_QAT = SRC_DECAY + DN_HEADS
SRC_KAT = SRC_QAT + ATT_WIDTH
SRC_VAT = SRC_KAT + ATT_KV_WIDTH
SRC_ZAT = SRC_VAT + ATT_KV_WIDTH
IN_DIM = SRC_ZAT + ATT_WIDTH

ROW_CHUNK = 8
ROW_GROUP = 8
PAIR = LANES // ATT_HEAD_DIM
N_PAIRS = ATT_Q_HEADS // PAIR
KEY_SLOTS = 4
KEYS_PAD = KEY_SLOTS * CHUNK
STEP_CHUNKS = 2
DN_SEQS_PER_STEP = 2


def _cparams(sem):
    return pltpu.CompilerParams(dimension_semantics=sem, vmem_limit_bytes=VMEM_LIMIT_BYTES)


def _dot(a, b):
    return jnp.dot(a, b, preferred_element_type=F32)


def _dot_nt(a, b):
    return lax.dot_general(a, b, (((1,), (1,)), ((), ())), preferred_element_type=F32)


def _dot_tn(a, b):
    return lax.dot_general(a, b, (((0,), (0,)), ((), ())), preferred_element_type=F32)


def _split(a):
    hi = a.astype(BF16)
    lo = (a - hi.astype(F32)).astype(BF16)
    return hi, lo


def _dot_split(a, b):
    ah, al = _split(a)
    bh, bl = _split(b)
    return _dot(ah, bh) + (_dot(ah, bl) + _dot(al, bh))


def _sigmoid(x):
    return 1.0 / (1.0 + jnp.exp(-x))


def _silu(x):
    return x * _sigmoid(x)


def _ada_kernel(c_ref, w_ref, b_ref, o_ref):
    s = _silu(c_ref[...]).astype(BF16)
    o_ref[...] = _dot(s, w_ref[...].astype(BF16)) + b_ref[...]


def _ada(c_all, w_ada, b_ada):
    rows, d = c_all.shape
    depth, _, n_out = w_ada.shape
    tn = 512
    return pl.pallas_call(
        _ada_kernel,
        grid=(depth, n_out // tn),
        in_specs=[
            pl.BlockSpec((rows, d), lambda l, j: (0, 0)),
            pl.BlockSpec((None, d, tn), lambda l, j: (l, 0, j)),
            pl.BlockSpec((None, 1, tn), lambda l, j: (l, 0, j)),
        ],
        out_specs=pl.BlockSpec((None, rows, tn), lambda l, j: (l, 0, j)),
        out_shape=jax.ShapeDtypeStruct((depth, rows, n_out), F32),
        compiler_params=_cparams(("arbitrary", "arbitrary")),
        name="ada",
    )(c_all, w_ada, b_ada.reshape(depth, 1, n_out))


def _rows_bcast(v):
    return jnp.broadcast_to(v[..., None, :], v.shape[:-1] + (ROW_CHUNK, v.shape[-1]))


def _row_chunks(n_rows, rows_per_batch, body):
    chunks_per_batch = rows_per_batch // ROW_CHUNK
    group = math.gcd(ROW_GROUP, n_rows // ROW_CHUNK)

    def step(r, carry):
        chunks = []
        for u in range(group):
            ci = r * group + u
            chunks.append((pl.ds(pl.multiple_of(ci * ROW_CHUNK, ROW_CHUNK), ROW_CHUNK), ci // chunks_per_batch))
        body(chunks)
        return carry

    lax.fori_loop(0, n_rows // (ROW_CHUNK * group), step, 0)


def _inv_rms(xs):
    return [lax.rsqrt(jnp.mean(x * x, axis=-1, keepdims=True) + NORM_EPS) for x in xs]


def _prenorm_kernel(x_ref, mod_ref, g_ref, h_ref, *, tt):
    def body(chunks):
        xs = [x_ref[rows, :] for rows, _ in chunks]
        for (rows, b), x, r in zip(chunks, xs, _inv_rms(xs)):
            h_ref[rows, :] = (x * r * g_ref[...] * (1.0 + mod_ref[b, 1]) + mod_ref[b, 0]).astype(BF16)

    _row_chunks(x_ref.shape[0], tt, body)


def _prenorm(x2, mod, g_pre, *, nb, tt):
    m, d = x2.shape
    tm = nb * tt
    tiles_per_batch = (m // mod.shape[0]) // tt
    return pl.pallas_call(
        functools.partial(_prenorm_kernel, tt=tt),
        grid=(m // tm,),
        in_specs=[
            pl.BlockSpec((tm, d), lambda i: (i, 0)),
            pl.BlockSpec((nb, 3, ROW_CHUNK, d), lambda i: (i // tiles_per_batch, 0, 0, 0)),
            pl.BlockSpec((ROW_CHUNK, d), lambda i: (0, 0)),
        ],
        out_specs=pl.BlockSpec((tm, d), lambda i: (i, 0)),
        out_shape=jax.ShapeDtypeStruct((m, d), BF16),
        compiler_params=_cparams(("arbitrary",)),
        name="prenorm",
    )(x2, mod, _rows_bcast(g_pre))


def _w_prep_kernel(w_ref, wdn_ref, wat_ref, wba_ref):
    def cast(lo, hi):
        return w_ref[:, lo:hi].astype(BF16)

    wdn_ref[...] = cast(0, DN_PART_DIM)
    wat_ref[:, COL_QAT:COL_QAT + ATT_WIDTH] = cast(SRC_QAT, SRC_QAT + ATT_WIDTH)
    wat_ref[:, COL_ZAT:COL_ZAT + ATT_WIDTH] = cast(SRC_ZAT, SRC_ZAT + ATT_WIDTH)
    wat_ref[:, COL_KAT:COL_KAT + ATT_KV_WIDTH] = cast(SRC_KAT, SRC_KAT + ATT_KV_WIDTH)
    wat_ref[:, COL_VAT:COL_VAT + ATT_KV_WIDTH] = cast(SRC_VAT, SRC_VAT + ATT_KV_WIDTH)
    wba_ref[...] = jnp.zeros_like(wba_ref)
    wba_ref[:, 0:2 * DN_HEADS] = cast(SRC_BETA, SRC_BETA + 2 * DN_HEADS)


def _w_prep(w_in):
    depth, d, in_dim = w_in.shape
    assert in_dim == IN_DIM
    tr = 128
    return pl.pallas_call(
        _w_prep_kernel,
        grid=(depth, d // tr),
        in_specs=[pl.BlockSpec((None, tr, in_dim), lambda l, i: (l, i, 0))],
        out_specs=[
            pl.BlockSpec((None, tr, DN_PART_DIM), lambda l, i: (l, i, 0)),
            pl.BlockSpec((None, tr, AT_PART_DIM), lambda l, i: (l, i, 0)),
            pl.BlockSpec((None, tr, BA_DIM), lambda l, i: (l, i, 0)),
        ],
        out_shape=[
            jax.ShapeDtypeStruct((depth, d, DN_PART_DIM), BF16),
            jax.ShapeDtypeStruct((depth, d, AT_PART_DIM), BF16),
            jax.ShapeDtypeStruct((depth, d, BA_DIM), BF16),
        ],
        compiler_params=_cparams(("arbitrary", "arbitrary")),
        name="w_prep",
    )(w_in)


def _in_proj_kernel(h_ref, w_ref, *rest):
    if len(rest) == 3:
        wba_ref, o_ref, oba_ref = rest

        @pl.when(pl.program_id(1) == 0)
        def _():
            oba_ref[...] = _dot(h_ref[...], wba_ref[...])
    else:
        (o_ref,) = rest
    o_ref[...] = _dot(h_ref[...], w_ref[...])


def _in_proj(h, w, layer, *, tm, tn, w_ba=None):
    m, d = h.shape
    n_out = w.shape[-1]
    in_specs = [
        pl.BlockSpec((tm, d), lambda i, j: (i, 0)),
        pl.BlockSpec((None, d, tn), lambda i, j: (layer, 0, j)),
    ]
    out_specs = [pl.BlockSpec((tm, tn), lambda i, j: (i, j))]
    out_shape = [jax.ShapeDtypeStruct((m, n_out), F32)]
    args = [h, w]
    if w_ba is not None:
        in_specs.append(pl.BlockSpec((None, d, BA_DIM), lambda i, j: (layer, 0, 0)))
        out_specs.append(pl.BlockSpec((tm, BA_DIM), lambda i, j: (i, 0)))
        out_shape.append(jax.ShapeDtypeStruct((m, BA_DIM), F32))
        args.append(w_ba)
    return pl.pallas_call(
        _in_proj_kernel,
        grid=(m // tm, n_out // tn),
        in_specs=in_specs,
        out_specs=out_specs,
        out_shape=out_shape,
        compiler_params=_cparams(("arbitrary", "arbitrary")),
        name="in_proj",
    )(*args)


def _deltanet_kernel(qkv_ref, z_ref, ba_ref, cbuf_ref, cw_ref, gpar_ref, ng_ref, s0_ref,
                     o_ref, s_ref, ext_ref, gct_ref, *, cs, nbk):
    c = pl.program_id(1)
    hd = DN_HEAD_DIM

    @pl.when(c == 0)
    def _():
        s_ref[...] = s0_ref[...]
        ext_ref[:, 0:SUBLANES, :] = cbuf_ref[...]

    ext_ref[:, SUBLANES:SUBLANES + cs, :] = qkv_ref[...]

    row = lax.broadcasted_iota(jnp.int32, (cs, LANES), 0)
    lane = lax.broadcasted_iota(jnp.int32, (cs, LANES), 1)
    beta_all, gc_all = [], []
    for b in range(nbk):
        ba = ba_ref[b]
        xg = ba + gpar_ref[1:2, :]
        softplus = jnp.maximum(xg, 0.0) + jnp.log(1.0 + jnp.exp(-jnp.abs(xg)))
        gc = -jnp.exp(gpar_ref[0:1, :]) * softplus
        shift = 1
        while shift < cs:
            gc = gc + jnp.where(row >= shift, pltpu.roll(gc, shift, axis=0), 0.0)
            shift *= 2
        gct_ref[b] = gc.T
        beta_all.append(_sigmoid(ba))
        gc_all.append(gc)

    ri = lax.broadcasted_iota(jnp.int32, (cs, cs), 0)
    ci = lax.broadcasted_iota(jnp.int32, (cs, cs), 1)
    causal = ri >= ci
    strict = ri > ci
    eye = jnp.where(ri == ci, 1.0, 0.0).astype(F32)
    n_doublings = int(math.log2(cs)) - 1

    def conv_silu(b, col0):
        cols = slice(col0, col0 + hd)
        first = SUBLANES - (CONV_W - 1)
        acc = ext_ref[b, first:first + cs, cols] * cw_ref[0:1, cols]
        for j in range(1, CONV_W):
            acc = acc + ext_ref[b, first + j:first + j + cs, cols] * cw_ref[j:j + 1, cols]
        return _silu(acc)

    units = [(b, h) for b in range(nbk) for h in range(DN_HEADS)]
    idx = range(len(units))
    q, k, v = [], [], []
    for b, h in units:
        qh = conv_silu(b, h * hd)
        kh = conv_silu(b, DN_WIDTH + h * hd)
        q.append(qh * lax.rsqrt(jnp.sum(qh * qh, axis=-1, keepdims=True) + NORM_EPS) * hd ** -0.5)
        k.append(kh * lax.rsqrt(jnp.sum(kh * kh, axis=-1, keepdims=True) + NORM_EPS))
        v.append(conv_silu(b, 2 * DN_WIDTH + h * hd))

    beta = [jnp.sum(jnp.where(lane == h, beta_all[b], 0.0), axis=-1, keepdims=True) for b, h in units]
    gcol = [jnp.sum(jnp.where(lane == DN_HEADS + h, gc_all[b], 0.0), axis=-1, keepdims=True) for b, h in units]
    grow = [gct_ref[b, DN_HEADS + h:DN_HEADS + h + 1, :] for b, h in units]
    glast = [g[:, cs - 1:cs] for g in grow]
    gamma = [jnp.where(causal, jnp.exp(gcol[u] - grow[u]), 0.0) for u in idx]
    ecol = [jnp.exp(g) for g in gcol]

    kb = [x.astype(BF16) for x in k]
    kq = [_dot_nt(jnp.concatenate([kb[u], q[u].astype(BF16)], axis=0), kb[u]) for u in idx]
    a_mat = [jnp.where(strict, beta[u] * kq[u][:cs] * gamma[u], 0.0) for u in idx]
    qk = [kq[u][cs:] * gamma[u] for u in idx]

    inv = [eye - a for a in a_mat]
    apow = [a.astype(BF16) for a in a_mat]
    for _ in range(n_doublings):
        apow = [_dot(a, a).astype(BF16) for a in apow]
        inv = [inv[u] + _dot(inv[u].astype(BF16), apow[u]) for u in idx]
    resid = [eye - inv[u] - _dot_split(a_mat[u], inv[u]) for u in idx]
    inv = [inv[u] + _dot(inv[u].astype(BF16), resid[u].astype(BF16)) for u in idx]

    rhs = [jnp.concatenate([beta[u] * v[u], beta[u] * ecol[u] * k[u]], axis=-1).astype(BF16) for u in idx]
    uw = [_dot(inv[u].astype(BF16), rhs[u]) for u in idx]

    state = [s_ref[b, h] for b, h in units]
    ws = [_dot(jnp.concatenate([uw[u][:, hd:], q[u] * ecol[u]], axis=0).astype(BF16), state[u].astype(BF16))
          for u in idx]
    v_new = [(uw[u][:, :hd] - ws[u][:cs]).astype(BF16) for u in idx]
    o = [ws[u][cs:] + _dot(qk[u].astype(BF16), v_new[u]) for u in idx]
    for u, (b, h) in enumerate(units):
        kdec = k[u] * jnp.exp(glast[u] - gcol[u])
        s_ref[b, h] = state[u] * jnp.exp(glast[u]) + _dot_tn(kdec.astype(BF16), v_new[u])
    for u, (b, h) in enumerate(units):
        oh = o[u] * lax.rsqrt(jnp.mean(o[u] * o[u], axis=-1, keepdims=True) + NORM_EPS) * ng_ref[...]
        oh = oh * _silu(z_ref[b, :, h * hd:(h + 1) * hd])
        o_ref[b, :, h * hd:(h + 1) * hd] = oh.astype(o_ref.dtype)
    ext_ref[:, 0:SUBLANES, :] = ext_ref[:, cs:cs + SUBLANES, :]


def _deltanet(main3, ba3, conv_buf8, conv_w, gpar, dn_norm, s0, *, cs, nbk):
    n, t, _ = main3.shape
    nc = t // cs
    hd = DN_HEAD_DIM
    return pl.pallas_call(
        functools.partial(_deltanet_kernel, cs=cs, nbk=nbk),
        grid=(n // nbk, nc),
        in_specs=[
            pl.BlockSpec((nbk, cs, DN_CONV_DIM), lambda b, c: (b, c, COL_QKV // DN_CONV_DIM)),
            pl.BlockSpec((nbk, cs, DN_WIDTH), lambda b, c: (b, c, COL_ZDN // DN_WIDTH)),
            pl.BlockSpec((nbk, cs, BA_DIM), lambda b, c: (b, c, 0)),
            pl.BlockSpec((nbk, SUBLANES, DN_CONV_DIM), lambda b, c: (b, 0, 0)),
            pl.BlockSpec((CONV_W, DN_CONV_DIM), lambda b, c: (0, 0)),
            pl.BlockSpec((2, BA_DIM), lambda b, c: (0, 0)),
            pl.BlockSpec((1, hd), lambda b, c: (0, 0)),
            pl.BlockSpec((nbk, DN_HEADS, hd, hd), lambda b, c: (b, 0, 0, 0)),
        ],
        out_specs=[
            pl.BlockSpec((nbk, cs, DN_WIDTH), lambda b, c: (b, c, 0)),
            pl.BlockSpec((nbk, DN_HEADS, hd, hd), lambda b, c: (b, 0, 0, 0)),
        ],
        out_shape=[
            jax.ShapeDtypeStruct((n, t, DN_WIDTH), BF16),
            jax.ShapeDtypeStruct((n, DN_HEADS, hd, hd), F32),
        ],
        scratch_shapes=[
            pltpu.VMEM((nbk, SUBLANES + cs, DN_CONV_DIM), F32),
            pltpu.VMEM((nbk, LANES, cs), F32),
        ],
        compiler_params=_cparams(("arbitrary", "arbitrary")),
        name="deltanet",
    )(main3, main3, ba3, conv_buf8, conv_w, gpar, dn_norm.reshape(1, hd), s0)


def _rope(x, cos, sin_signed):
    lane = lax.broadcasted_iota(jnp.int32, x.shape, 1)
    half = ATT_HEAD_DIM // 2
    swapped = jnp.where((lane % ATT_HEAD_DIM) < half,
                        pltpu.roll(x, LANES - half, axis=1), pltpu.roll(x, half, axis=1))
    return x * cos + swapped * sin_signed


def _split_heads(tile):
    lane = lax.broadcasted_iota(jnp.int32, tile.shape, 1)
    lo = lane < ATT_HEAD_DIM
    rolled = pltpu.roll(tile, ATT_HEAD_DIM, axis=1)
    zero = jnp.zeros_like(tile)
    return (jnp.where(lo, tile, zero), jnp.where(lo, zero, rolled),
            jnp.where(lo, rolled, zero), jnp.where(lo, zero, tile))


def _store_keys(dst_ref, tile_idx, row0, rows, tile):
    a_top, a_bot, b_top, b_bot = _split_heads(tile)
    ja = PAIR * tile_idx
    dst_ref[ja, pl.ds(row0, rows), :] = a_top.astype(BF16)
    dst_ref[ja, pl.ds(KEYS_PAD + row0, rows), :] = a_bot.astype(BF16)
    dst_ref[ja + 1, pl.ds(row0, rows), :] = b_top.astype(BF16)
    dst_ref[ja + 1, pl.ds(KEYS_PAD + row0, rows), :] = b_bot.astype(BF16)


def _attend_pairs(q_ref, z_ref, cos, sin_signed, sinks_ref, k2_ref, v2_ref, o_ref, valid, group=8):
    rows = q_ref.shape[0]
    lane = lax.broadcasted_iota(jnp.int32, (rows, LANES), 1)
    lo = lane < ATT_HEAD_DIM
    neg_inf = jnp.full((rows, PAIR * KEYS_PAD), -jnp.inf, F32)

    def kv_of(p):
        return p // (N_PAIRS // ATT_KV_HEADS)

    def scores(p):
        cols = slice(p * LANES, (p + 1) * LANES)
        q = _rope(q_ref[:, cols], cos, sin_signed) * ATT_HEAD_DIM ** -0.5
        return _dot_nt(q.astype(BF16), k2_ref[kv_of(p)])

    def softmax(pairs, s_list):
        heads = [(p, hh) for p in pairs for hh in range(PAIR)]
        sh = [jnp.where(valid, s, neg_inf)[:, hh * KEYS_PAD:(hh + 1) * KEYS_PAD]
              for s in s_list for hh in range(PAIR)]
        sink = [sinks_ref[PAIR * p + hh] for p, hh in heads]
        m = [jnp.maximum(jnp.max(x, axis=-1, keepdims=True), sk) for x, sk in zip(sh, sink)]
        ph = [jnp.exp(x - mx) for x, mx in zip(sh, m)]
        probs = [jnp.concatenate(ph[PAIR * i:PAIR * (i + 1)], axis=-1).astype(BF16) for i in range(len(pairs))]
        inv = [1.0 / (jnp.sum(x, axis=-1, keepdims=True) + jnp.exp(sk - mx)) for x, sk, mx in zip(ph, sink, m)]
        inv_den = [jnp.where(lo, inv[PAIR * i], inv[PAIR * i + 1]) for i in range(len(pairs))]
        return probs, inv_den

    def finish(p, probs, inv_den):
        cols = slice(p * LANES, (p + 1) * LANES)
        o = _dot(probs, v2_ref[kv_of(p)]) * inv_den
        o_ref[:, cols] = (o * _silu(z_ref[:, cols])).astype(o_ref.dtype)

    groups = [range(g, g + group) for g in range(0, N_PAIRS, group)]
    s_next = [scores(p) for p in groups[0]]
    for gi, pairs in enumerate(groups):
        s_cur = s_next
        if gi + 1 < len(groups):
            s_next = [scores(p) for p in groups[gi + 1]]
        probs, inv_den = softmax(pairs, s_cur)
        for i, p in enumerate(pairs):
            finish(p, probs[i], inv_den[i])


def _swa_prompt_kernel(sinks_ref, q_ref, z_ref, k_ref, v_ref, cos_ref, sin_ref,
                       o_ref, kout_ref, k2_ref, v2_ref, *, n_steps):
    s = pl.program_id(1)
    rows = STEP_CHUNKS * CHUNK

    @pl.when(s == 0)
    def _():
        k2_ref[...] = jnp.zeros_like(k2_ref)
        v2_ref[...] = jnp.zeros_like(v2_ref)

    cos = cos_ref[...]
    sin_signed = sin_ref[...]
    half = s % (KEY_SLOTS // STEP_CHUNKS)
    row0 = pl.multiple_of(half * rows, rows)
    for ti in range(ATT_KV_WIDTH // LANES):
        k_rot = _rope(k_ref[:, ti * LANES:(ti + 1) * LANES], cos, sin_signed)
        _store_keys(k2_ref, ti, row0, rows, k_rot)
        _store_keys(v2_ref, ti, row0, rows, v_ref[:, ti * LANES:(ti + 1) * LANES])

        @pl.when(s == n_steps - 1)
        def _():
            kout_ref[0, :, ti * LANES:(ti + 1) * LANES] = k_rot

    shape = (rows, PAIR * KEYS_PAD)
    key_slot = (lax.broadcasted_iota(jnp.int32, shape, 1) % KEYS_PAD) // CHUNK
    rel = key_slot % STEP_CHUNKS - jnp.where(key_slot // STEP_CHUNKS == half, 0, STEP_CHUNKS)
    q_chunk = lax.broadcasted_iota(jnp.int32, shape, 0) // CHUNK
    valid = (rel <= q_chunk) & (rel >= q_chunk - WINDOW // CHUNK) & (s * STEP_CHUNKS + rel >= 0)
    _attend_pairs(q_ref, z_ref, cos, sin_signed, sinks_ref, k2_ref, v2_ref, o_ref, valid)


def _swa_prompt(main, sinks, cos_t, sin_t, *, n, t):
    rows = STEP_CHUNKS * CHUNK
    assert t % rows == 0 and rows == WINDOW
    ns = t // rows
    kvw = ATT_KV_WIDTH
    grid_spec = pltpu.PrefetchScalarGridSpec(
        num_scalar_prefetch=1,
        grid=(n, ns),
        in_specs=[
            pl.BlockSpec((rows, ATT_WIDTH), lambda b, c, s: (b * ns + c, COL_QAT // ATT_WIDTH)),
            pl.BlockSpec((rows, ATT_WIDTH), lambda b, c, s: (b * ns + c, COL_ZAT // ATT_WIDTH)),
            pl.BlockSpec((rows, kvw), lambda b, c, s: (b * ns + c, COL_KAT // kvw)),
            pl.BlockSpec((rows, kvw), lambda b, c, s: (b * ns + c, COL_VAT // kvw)),
            pl.BlockSpec((rows, LANES), lambda b, c, s: (c, 0)),
            pl.BlockSpec((rows, LANES), lambda b, c, s: (c, 0)),
        ],
        out_specs=[
            pl.BlockSpec((rows, ATT_WIDTH), lambda b, c, s: (b * ns + c, 0)),
            pl.BlockSpec((1, WINDOW, kvw), lambda b, c, s: (b, 0, 0)),
        ],
        scratch_shapes=[
            pltpu.VMEM((ATT_KV_HEADS, PAIR * KEYS_PAD, LANES), BF16),
            pltpu.VMEM((ATT_KV_HEADS, PAIR * KEYS_PAD, LANES), BF16),
        ],
    )
    return pl.pallas_call(
        functools.partial(_swa_prompt_kernel, n_steps=ns),
        grid_spec=grid_spec,
        out_shape=[
            jax.ShapeDtypeStruct((n * t, ATT_WIDTH), BF16),
            jax.ShapeDtypeStruct((n, WINDOW, kvw), F32),
        ],
        compiler_params=_cparams(("arbitrary", "arbitrary")),
        name="swa_prompt",
    )(sinks, main, main, main, main, cos_t, sin_t)


def _swa_sample_kernel(sinks_ref, q_ref, z_ref, k_ref, v_ref, ck_ref, cv_ref, cos_ref, sin_ref,
                       o_ref, kout_ref, k2_ref, v2_ref, *, t, past):
    k2_ref[...] = jnp.zeros_like(k2_ref)
    v2_ref[...] = jnp.zeros_like(v2_ref)
    cos = cos_ref[...]
    sin_signed = sin_ref[...]
    for ti in range(ATT_KV_WIDTH // LANES):
        cols = slice(ti * LANES, (ti + 1) * LANES)
        k_rot = _rope(k_ref[:, cols], cos, sin_signed)
        _store_keys(k2_ref, ti, 0, past, ck_ref[0, :, cols])
        _store_keys(k2_ref, ti, past, t, k_rot)
        _store_keys(v2_ref, ti, 0, past, cv_ref[0, :, cols])
        _store_keys(v2_ref, ti, past, t, v_ref[:, cols])
        kout_ref[0, 0:past - t, cols] = ck_ref[0, t:past, cols]
        kout_ref[0, past - t:past, cols] = k_rot
    key_pos = lax.broadcasted_iota(jnp.int32, (t, PAIR * KEYS_PAD), 1) % KEYS_PAD
    valid = key_pos < past + t
    _attend_pairs(q_ref, z_ref, cos, sin_signed, sinks_ref, k2_ref, v2_ref, o_ref, valid)


def _swa_sample(main, sinks, cache_k, cache_v, cos_t, sin_t, *, n, t):
    past = cache_k.shape[1]
    assert past + t <= KEYS_PAD and t <= past
    kvw = ATT_KV_WIDTH
    grid_spec = pltpu.PrefetchScalarGridSpec(
        num_scalar_prefetch=1,
        grid=(n,),
        in_specs=[
            pl.BlockSpec((t, ATT_WIDTH), lambda b, s: (b, COL_QAT // ATT_WIDTH)),
            pl.BlockSpec((t, ATT_WIDTH), lambda b, s: (b, COL_ZAT // ATT_WIDTH)),
            pl.BlockSpec((t, kvw), lambda b, s: (b, COL_KAT // kvw)),
            pl.BlockSpec((t, kvw), lambda b, s: (b, COL_VAT // kvw)),
            pl.BlockSpec((1, past, kvw), lambda b, s: (b, 0, 0)),
            pl.BlockSpec((1, past, kvw), lambda b, s: (b, 0, 0)),
            pl.BlockSpec((t, LANES), lambda b, s: (0, 0)),
            pl.BlockSpec((t, LANES), lambda b, s: (0, 0)),
        ],
        out_specs=[
            pl.BlockSpec((t, ATT_WIDTH), lambda b, s: (b, 0)),
            pl.BlockSpec((1, past, kvw), lambda b, s: (b, 0, 0)),
        ],
        scratch_shapes=[
            pltpu.VMEM((ATT_KV_HEADS, PAIR * KEYS_PAD, LANES), BF16),
            pltpu.VMEM((ATT_KV_HEADS, PAIR * KEYS_PAD, LANES), BF16),
        ],
    )
    return pl.pallas_call(
        functools.partial(_swa_sample_kernel, t=t, past=past),
        grid_spec=grid_spec,
        out_shape=[
            jax.ShapeDtypeStruct((n * t, ATT_WIDTH), BF16),
            jax.ShapeDtypeStruct((n, past, kvw), F32),
        ],
        compiler_params=_cparams(("arbitrary",)),
        name="swa_sample",
    )(sinks, main, main, main, main, cache_k, cache_v, cos_t, sin_t)


def _out_proj_kernel(odn_ref, oat_ref, w_ref, x_ref, mod_ref, gpost_ref, *rest, tt, nk, emit_h):
    if emit_h:
        modn_ref, gpre_ref, xo_ref, h_ref = rest
    else:
        (xo_ref,) = rest
    kk = pl.program_id(1)
    half = nk // 2

    @pl.when(kk == 0)
    def _():
        xo_ref[...] = jnp.zeros_like(xo_ref)

    @pl.when(kk < half)
    def _():
        xo_ref[...] += _dot(odn_ref[...], w_ref[...])

    @pl.when(kk >= half)
    def _():
        xo_ref[...] += _dot(oat_ref[...], w_ref[...])

    @pl.when(kk == nk - 1)
    def _():
        def body(chunks):
            ys = [xo_ref[rows, :] for rows, _ in chunks]
            x_new = [x_ref[rows, :] + mod_ref[b, 2] * (y * r * gpost_ref[...])
                     for (rows, b), y, r in zip(chunks, ys, _inv_rms(ys))]
            for (rows, _), xn in zip(chunks, x_new):
                xo_ref[rows, :] = xn
            if emit_h:
                for (rows, b), xn, r in zip(chunks, x_new, _inv_rms(x_new)):
                    h_ref[rows, :] = (xn * r * gpre_ref[...] * (1.0 + modn_ref[b, 1])
                                      + modn_ref[b, 0]).astype(BF16)

        _row_chunks(xo_ref.shape[0], tt, body)


def _out_proj(o_dn, o_at, w_out_b, layer, x2, mod, g_post, mod_next, g_pre_next, *, nb, tt, tk):
    m, d = x2.shape
    tm = nb * tt
    nk = (DN_WIDTH + ATT_WIDTH) // tk
    half = nk // 2
    tiles_per_batch = (m // mod.shape[0]) // tt
    emit_h = mod_next is not None
    mod_spec = pl.BlockSpec((nb, 3, ROW_CHUNK, d), lambda i, k: (i // tiles_per_batch, 0, 0, 0))
    vec_spec = pl.BlockSpec((ROW_CHUNK, d), lambda i, k: (0, 0))
    row_spec = pl.BlockSpec((tm, d), lambda i, k: (i, 0))
    in_specs = [
        pl.BlockSpec((tm, tk), lambda i, k: (i, jnp.minimum(k, half - 1))),
        pl.BlockSpec((tm, tk), lambda i, k: (i, jnp.maximum(k - half, 0))),
        pl.BlockSpec((None, tk, d), lambda i, k: (layer, k, 0)),
        row_spec, mod_spec, vec_spec,
    ]
    args = [o_dn, o_at, w_out_b, x2, mod, _rows_bcast(g_post)]
    out_specs = [row_spec]
    out_shape = [jax.ShapeDtypeStruct((m, d), F32)]
    if emit_h:
        in_specs += [mod_spec, vec_spec]
        args += [mod_next, _rows_bcast(g_pre_next)]
        out_specs.append(row_spec)
        out_shape.append(jax.ShapeDtypeStruct((m, d), BF16))
    outs = pl.pallas_call(
        functools.partial(_out_proj_kernel, tt=tt, nk=nk, emit_h=emit_h),
        grid=(m // tm, nk),
        in_specs=in_specs,
        out_specs=out_specs,
        out_shape=out_shape,
        compiler_params=_cparams(("arbitrary", "arbitrary")),
        name="out_proj",
    )(*args)
    return (outs[0], outs[1]) if emit_h else (outs[0], None)


def _rope_tables(pos):
    half = ATT_HEAD_DIM // 2
    inv = ROPE_THETA ** (-jnp.arange(half, dtype=F32) / half)
    ang = pos.astype(F32)[:, None] * inv[None, :]
    cos = jnp.cos(ang)
    sin = jnp.sin(ang)
    reps = LANES // ATT_HEAD_DIM
    return jnp.tile(jnp.concatenate([cos, cos], axis=-1), (1, reps)), \
        jnp.tile(jnp.concatenate([-sin, sin], axis=-1), (1, reps))


def _layer(x2, h, mod, mod_next, conv_buf, s0, cache_k, cache_v, pos, lw, layer, *, n, t, prompt):
    (g_post, g_pre_next, w_dn, w_at, w_ba, conv_w, gpar, dn_norm, sinks, w_out_b) = lw
    if prompt:
        nb, tt, tm_in, cs = 1, 512, min(1024, n * t), CHUNK
    else:
        nb, tt, tm_in, cs = n, t, n * t, min(CHUNK, t)
    (main_dn,) = _in_proj(h, w_dn, layer, tm=tm_in, tn=1024)
    main, ba = _in_proj(h, w_at, layer, tm=tm_in, tn=768, w_ba=w_ba)

    conv_buf8 = jnp.pad(conv_buf, ((0, 0), (SUBLANES - (CONV_W - 1), 0), (0, 0)))
    main_dn3 = main_dn.reshape(n, t, DN_PART_DIM)
    o_dn, s_new = _deltanet(main_dn3, ba.reshape(n, t, BA_DIM), conv_buf8, conv_w, gpar, dn_norm, s0,
                            cs=cs, nbk=DN_SEQS_PER_STEP)
    o_dn = o_dn.reshape(n * t, DN_WIDTH)
    conv_new = main_dn3[:, t - (CONV_W - 1):, COL_QKV:COL_QKV + DN_CONV_DIM]

    cos_t, sin_t = _rope_tables(pos)
    v_rows = main.reshape(n, t, AT_PART_DIM)[:, :, COL_VAT:COL_VAT + ATT_KV_WIDTH]
    if prompt:
        o_at, k_new = _swa_prompt(main, sinks, cos_t, sin_t, n=n, t=t)
        v_new = v_rows[:, t - WINDOW:]
    else:
        past = cache_k.shape[1]
        ck = cache_k.reshape(n, past, ATT_KV_WIDTH)
        cv = cache_v.reshape(n, past, ATT_KV_WIDTH)
        o_at, k_new = _swa_sample(main, sinks, ck, cv, cos_t, sin_t, n=n, t=t)
        v_new = jnp.concatenate([cv, v_rows], axis=1)[:, -past:]
    rows = k_new.shape[1]
    k_new = k_new.reshape(n, rows, ATT_KV_HEADS, ATT_HEAD_DIM)
    v_new = v_new.reshape(n, rows, ATT_KV_HEADS, ATT_HEAD_DIM)

    x_new, h_next = _out_proj(o_dn, o_at, w_out_b, layer, x2, mod, g_post, mod_next, g_pre_next,
                              nb=nb, tt=tt, tk=512)
    return x_new, h_next, conv_new, s_new, k_new, v_new


def kernel(x_prompt, x_sample, state_conv, state_dn, cache_k, cache_v, c_prompt, c_sample,
           w_ada, b_ada, g_pre, g_post, w_in, conv_w, a_log, dt_bias, dn_norm, sinks, w_out):
    depth = w_in.shape[0]
    bp, tp, d = x_prompt.shape
    bs, ts, _ = x_sample.shape
    pos_p = jnp.arange(tp)
    pos_s = PAST_LEN + jnp.arange(ts)

    w_dn, w_at, w_ba = _w_prep(w_in)
    w_out_b = w_out.astype(BF16)
    pad_rows = (-(bp + bs)) % (2 * SUBLANES)
    c_all = jnp.concatenate([c_prompt, c_sample, jnp.zeros((pad_rows, d), F32)], axis=0)
    gate_pad = jnp.zeros((depth, BA_DIM - 2 * DN_HEADS), F32)
    head_pad = jnp.zeros((depth, DN_HEADS), F32)
    gpar = jnp.stack([jnp.concatenate([head_pad, a_log, gate_pad], axis=-1),
                      jnp.concatenate([head_pad, dt_bias, gate_pad], axis=-1)], axis=1)

    zero_conv = jnp.zeros((bp, CONV_W - 1, DN_CONV_DIM), F32)
    zero_state = jnp.zeros((bp, DN_HEADS, DN_HEAD_DIM, DN_HEAD_DIM), F32)

    ada = _ada(c_all, w_ada, b_ada)
    mod_p = _rows_bcast(ada[:, :bp].reshape(depth, bp, 3, d))
    mod_s = _rows_bcast(ada[:, bp:bp + bs].reshape(depth, bs, 3, d))

    xp = x_prompt.reshape(bp * tp, d)
    xs = x_sample.reshape(bs * ts, d)
    hp = _prenorm(xp, mod_p[0], g_pre[0], nb=1, tt=512)
    hs = _prenorm(xs, mod_s[0], g_pre[0], nb=bs, tt=ts)
    outs_p, outs_s = [], []
    for l in range(depth):
        last = l == depth - 1
        lw = (g_post[l], None if last else g_pre[l + 1], w_dn, w_at, w_ba, conv_w[l], gpar[l], dn_norm[l],
              sinks[l], w_out_b)
        xp, hp, *rest_p = _layer(xp, hp, mod_p[l], None if last else mod_p[l + 1], zero_conv, zero_state,
                                 None, None, pos_p, lw, l, n=bp, t=tp, prompt=True)
        xs, hs, *rest_s = _layer(xs, hs, mod_s[l], None if last else mod_s[l + 1], state_conv[l], state_dn[l],
                                 cache_k[l], cache_v[l], pos_s, lw, l, n=bs, t=ts, prompt=False)
        outs_p.append(rest_p)
        outs_s.append(rest_s)
    xp = xp.reshape(bp, tp, d)
    xs = xs.reshape(bs, ts, d)

    def stack(outs, i):
        return jnp.stack([o[i] for o in outs])

    return (xp, xs,
            stack(outs_p, 0), stack(outs_p, 1), stack(outs_p, 2), stack(outs_p, 3),
            stack(outs_s, 0), stack(outs_s, 1), stack(outs_s, 2), stack(outs_s, 3))
```

```python
import functools
import math

import jax
import jax.numpy as jnp
from jax import lax
from jax.experimental import pallas as pl
from jax.experimental.pallas import tpu as pltpu

F32 = jnp.float32
BF16 = jnp.bfloat16

D_MODEL = 4096
CHUNK = 64
DN_WIDTH = D_MODEL // 2
DN_HEAD_DIM = 128
DN_HEADS = DN_WIDTH // DN_HEAD_DIM
CONV_W = 4
DN_CONV_DIM = 3 * DN_WIDTH
ATT_WIDTH = D_MODEL - DN_WIDTH
ATT_HEAD_DIM = 64
ATT_Q_HEADS = ATT_WIDTH // ATT_HEAD_DIM
ATT_KV_HEADS = ATT_Q_HEADS // 8
ATT_KV_WIDTH = ATT_KV_HEADS * ATT_HEAD_DIM
WINDOW = 128
ROPE_THETA = 10000.0
NORM_EPS = 1e-6
PAST_LEN = 2048

LANES = 128
SUBLANES = 8
VMEM_LIMIT_BYTES = 56 * 1024 * 1024

COL_QKV = 0
COL_ZDN = COL_QKV + DN_CONV_DIM
DN_PART_DIM = COL_ZDN + DN_WIDTH
COL_QAT = 0
COL_ZAT = COL_QAT + ATT_WIDTH
COL_KAT = COL_ZAT + ATT_WIDTH
COL_VAT = COL_KAT + ATT_KV_WIDTH
AT_PART_DIM = COL_VAT + ATT_KV_WIDTH
BA_DIM = LANES
SRC_BETA = DN_PART_DIM
SRC_DECAY = SRC_BETA + DN_HEADS
SRC_QAT = SRC_DECAY + DN_HEADS
SRC_KAT = SRC_QAT + ATT_WIDTH
SRC_VAT = SRC_KAT + ATT_KV_WIDTH
SRC_ZAT = SRC_VAT + ATT_KV_WIDTH
IN_DIM = SRC_ZAT + ATT_WIDTH

ROW_CHUNK = 8
ROW_GROUP = 8
PAIR = LANES // ATT_HEAD_DIM
N_PAIRS = ATT_Q_HEADS // PAIR
KEY_SLOTS = 4
KEYS_PAD = KEY_SLOTS * CHUNK
STEP_CHUNKS = 2
DN_SEQS_PER_STEP = 2


def _cparams(sem):
    return pltpu.CompilerParams(dimension_semantics=sem, vmem_limit_bytes=VMEM_LIMIT_BYTES)


def _dot(a, b):
    return jnp.dot(a, b, preferred_element_type=F32)


def _dot_nt(a, b):
    return lax.dot_general(a, b, (((1,), (1,)), ((), ())), preferred_element_type=F32)


def _dot_tn(a, b):
    return lax.dot_general(a, b, (((0,), (0,)), ((), ())), preferred_element_type=F32)


def _split(a):
    hi = a.astype(BF16)
    lo = (a - hi.astype(F32)).astype(BF16)
    return hi, lo


def _dot_split(a, b):
    ah, al = _split(a)
    bh, bl = _split(b)
    return _dot(ah, bh) + (_dot(ah, bl) + _dot(al, bh))


def _sigmoid(x):
    return 1.0 / (1.0 + jnp.exp(-x))


def _silu(x):
    return x * _sigmoid(x)


def _ada_kernel(c_ref, w_ref, b_ref, o_ref):
    s = _silu(c_ref[...]).astype(BF16)
    o_ref[...] = _dot(s, w_ref[...].astype(BF16)) + b_ref[...]


def _ada(c_all, w_ada, b_ada):
    rows, d = c_all.shape
    depth, _, n_out = w_ada.shape
    tn = 512
    return pl.pallas_call(
        _ada_kernel,
        grid=(depth, n_out // tn),
        in_specs=[
            pl.BlockSpec((rows, d), lambda l, j: (0, 0)),
            pl.BlockSpec((None, d, tn), lambda l, j: (l, 0, j)),
            pl.BlockSpec((None, 1, tn), lambda l, j: (l, 0, j)),
        ],
        out_specs=pl.BlockSpec((None, rows, tn), lambda l, j: (l, 0, j)),
        out_shape=jax.ShapeDtypeStruct((depth, rows, n_out), F32),
        compiler_params=_cparams(("arbitrary", "arbitrary")),
        name="ada",
    )(c_all, w_ada, b_ada.reshape(depth, 1, n_out))


def _rows_bcast(v):
    return jnp.broadcast_to(v[..., None, :], v.shape[:-1] + (ROW_CHUNK, v.shape[-1]))


def _row_chunks(n_rows, rows_per_batch, body):
    chunks_per_batch = rows_per_batch // ROW_CHUNK
    group = math.gcd(ROW_GROUP, n_rows // ROW_CHUNK)

    def step(r, carry):
        chunks = []
        for u in range(group):
            ci = r * group + u
            chunks.append((pl.ds(pl.multiple_of(ci * ROW_CHUNK, ROW_CHUNK), ROW_CHUNK), ci // chunks_per_batch))
        body(chunks)
        return carry

    lax.fori_loop(0, n_rows // (ROW_CHUNK * group), step, 0)


def _inv_rms(xs):
    return [lax.rsqrt(jnp.mean(x * x, axis=-1, keepdims=True) + NORM_EPS) for x in xs]


def _prenorm_kernel(x_ref, mod_ref, g_ref, h_ref, *, tt):
    def body(chunks):
        xs = [x_ref[rows, :] for rows, _ in chunks]
        for (rows, b), x, r in zip(chunks, xs, _inv_rms(xs)):
            h_ref[rows, :] = (x * r * g_ref[...] * (1.0 + mod_ref[b, 1]) + mod_ref[b, 0]).astype(BF16)

    _row_chunks(x_ref.shape[0], tt, body)


def _prenorm(x2, mod, g_pre, *, nb, tt):
    m, d = x2.shape
    tm = nb * tt
    tiles_per_batch = (m // mod.shape[0]) // tt
    return pl.pallas_call(
        functools.partial(_prenorm_kernel, tt=tt),
        grid=(m // tm,),
        in_specs=[
            pl.BlockSpec((tm, d), lambda i: (i, 0)),
            pl.BlockSpec((nb, 3, ROW_CHUNK, d), lambda i: (i // tiles_per_batch, 0, 0, 0)),
            pl.BlockSpec((ROW_CHUNK, d), lambda i: (0, 0)),
        ],
        out_specs=pl.BlockSpec((tm, d), lambda i: (i, 0)),
        out_shape=jax.ShapeDtypeStruct((m, d), BF16),
        compiler_params=_cparams(("arbitrary",)),
        name="prenorm",
    )(x2, mod, _rows_bcast(g_pre))


def _cast_kernel(starts_ref, w_ref, o_ref, *, valid_rows):
    w = w_ref[0]
    if valid_rows < w.shape[0]:
        row = lax.broadcasted_iota(jnp.int32, w.shape, 0)
        w = jnp.where(row < valid_rows, w, 0.0)
    o_ref[...] = w.astype(BF16)


def _cast_rows(w_t, starts, tr, valid_rows=None):
    depth, _, d = w_t.shape
    n_blk = len(starts)
    grid_spec = pltpu.PrefetchScalarGridSpec(
        num_scalar_prefetch=1,
        grid=(depth, n_blk),
        in_specs=[pl.BlockSpec((pl.Element(1), pl.Element(tr), pl.Element(d)),
                               lambda l, i, st: (l, pl.multiple_of(st[i], 2 * SUBLANES), 0))],
        out_specs=pl.BlockSpec((None, tr, d), lambda l, i, st: (l, i, 0)),
    )
    return pl.pallas_call(
        functools.partial(_cast_kernel, valid_rows=tr if valid_rows is None else valid_rows),
        grid_spec=grid_spec,
        out_shape=jax.ShapeDtypeStruct((depth, n_blk * tr, d), BF16),
        compiler_params=_cparams(("arbitrary", "arbitrary")),
        name="w_prep",
    )(jnp.asarray(starts, jnp.int32), w_t)


def _w_prep(w_in):
    assert w_in.shape[-1] == IN_DIM
    w_t = jnp.swapaxes(w_in, 1, 2)
    tr = ATT_KV_WIDTH
    w_dn = _cast_rows(w_t, list(range(0, DN_PART_DIM, tr)), tr)
    at_rows = ([SRC_QAT + r for r in range(0, ATT_WIDTH, tr)] + [SRC_ZAT + r for r in range(0, ATT_WIDTH, tr)]
               + [SRC_KAT, SRC_VAT])
    w_at = _cast_rows(w_t, at_rows, tr)
    w_ba = _cast_rows(w_t, [SRC_BETA], BA_DIM, valid_rows=2 * DN_HEADS)
    return w_dn, w_at, w_ba


def _in_proj_kernel(h_ref, w_ref, *rest):
    if len(rest) == 3:
        wba_ref, o_ref, oba_ref = rest

        @pl.when(pl.program_id(1) == 0)
        def _():
            oba_ref[...] = _dot_nt(h_ref[...], wba_ref[...])
    else:
        (o_ref,) = rest
    o_ref[...] = _dot_nt(h_ref[...], w_ref[...])


def _in_proj(h, w, layer, *, tm, tn, w_ba=None):
    m, d = h.shape
    n_out = w.shape[1]
    in_specs = [
        pl.BlockSpec((tm, d), lambda i, j: (i, 0)),
        pl.BlockSpec((None, tn, d), lambda i, j: (layer, j, 0)),
    ]
    out_specs = [pl.BlockSpec((tm, tn), lambda i, j: (i, j))]
    out_shape = [jax.ShapeDtypeStruct((m, n_out), F32)]
    args = [h, w]
    if w_ba is not None:
        in_specs.append(pl.BlockSpec((None, BA_DIM, d), lambda i, j: (layer, 0, 0)))
        out_specs.append(pl.BlockSpec((tm, BA_DIM), lambda i, j: (i, 0)))
        out_shape.append(jax.ShapeDtypeStruct((m, BA_DIM), F32))
        args.append(w_ba)
    return pl.pallas_call(
        _in_proj_kernel,
        grid=(m // tm, n_out // tn),
        in_specs=in_specs,
        out_specs=out_specs,
        out_shape=out_shape,
        compiler_params=_cparams(("arbitrary", "arbitrary")),
        name="in_proj",
    )(*args)


def _deltanet_kernel(qkv_ref, z_ref, ba_ref, cbuf_ref, cw_ref, gpar_ref, ng_ref, s0_ref,
                     o_ref, s_ref, ext_ref, gct_ref, *, cs, nbk):
    c = pl.program_id(1)
    hd = DN_HEAD_DIM

    @pl.when(c == 0)
    def _():
        s_ref[...] = s0_ref[...]
        ext_ref[:, 0:SUBLANES, :] = cbuf_ref[...]

    ext_ref[:, SUBLANES:SUBLANES + cs, :] = qkv_ref[...]

    row = lax.broadcasted_iota(jnp.int32, (cs, LANES), 0)
    lane = lax.broadcasted_iota(jnp.int32, (cs, LANES), 1)
    beta_all, gc_all = [], []
    for b in range(nbk):
        ba = ba_ref[b]
        xg = ba + gpar_ref[1:2, :]
        softplus = jnp.maximum(xg, 0.0) + jnp.log(1.0 + jnp.exp(-jnp.abs(xg)))
        gc = -jnp.exp(gpar_ref[0:1, :]) * softplus
        shift = 1
        while shift < cs:
            gc = gc + jnp.where(row >= shift, pltpu.roll(gc, shift, axis=0), 0.0)
            shift *= 2
        gct_ref[b] = gc.T
        beta_all.append(_sigmoid(ba))
        gc_all.append(gc)

    ri = lax.broadcasted_iota(jnp.int32, (cs, cs), 0)
    ci = lax.broadcasted_iota(jnp.int32, (cs, cs), 1)
    causal = ri >= ci
    strict = ri > ci
    eye = jnp.where(ri == ci, 1.0, 0.0).astype(F32)
    n_doublings = int(math.log2(cs)) - 1

    def conv_silu(b, col0):
        cols = slice(col0, col0 + hd)
        first = SUBLANES - (CONV_W - 1)
        acc = ext_ref[b, first:first + cs, cols] * cw_ref[0:1, cols]
        for j in range(1, CONV_W):
            acc = acc + ext_ref[b, first + j:first + j + cs, cols] * cw_ref[j:j + 1, cols]
        return _silu(acc)

    units = [(b, h) for b in range(nbk) for h in range(DN_HEADS)]
    idx = range(len(units))
    q, k, v = [], [], []
    for b, h in units:
        qh = conv_silu(b, h * hd)
        kh = conv_silu(b, DN_WIDTH + h * hd)
        q.append(qh * lax.rsqrt(jnp.sum(qh * qh, axis=-1, keepdims=True) + NORM_EPS) * hd ** -0.5)
        k.append(kh * lax.rsqrt(jnp.sum(kh * kh, axis=-1, keepdims=True) + NORM_EPS))
        v.append(conv_silu(b, 2 * DN_WIDTH + h * hd))

    beta = [jnp.sum(jnp.where(lane == h, beta_all[b], 0.0), axis=-1, keepdims=True) for b, h in units]
    gcol = [jnp.sum(jnp.where(lane == DN_HEADS + h, gc_all[b], 0.0), axis=-1, keepdims=True) for b, h in units]
    grow = [gct_ref[b, DN_HEADS + h:DN_HEADS + h + 1, :] for b, h in units]
    glast = [g[:, cs - 1:cs] for g in grow]
    gamma = [jnp.where(causal, jnp.exp(gcol[u] - grow[u]), 0.0) for u in idx]
    ecol = [jnp.exp(g) for g in gcol]

    kb = [x.astype(BF16) for x in k]
    kq = [_dot_nt(jnp.concatenate([kb[u], q[u].astype(BF16)], axis=0), kb[u]) for u in idx]
    a_mat = [jnp.where(strict, beta[u] * kq[u][:cs] * gamma[u], 0.0) for u in idx]
    qk = [kq[u][cs:] * gamma[u] for u in idx]

    inv = [eye - a for a in a_mat]
    apow = [a.astype(BF16) for a in a_mat]
    for _ in range(n_doublings):
        apow = [_dot(a, a).astype(BF16) for a in apow]
        inv = [inv[u] + _dot(inv[u].astype(BF16), apow[u]) for u in idx]
    resid = [eye - inv[u] - _dot_split(a_mat[u], inv[u]) for u in idx]
    inv = [inv[u] + _dot(inv[u].astype(BF16), resid[u].astype(BF16)) for u in idx]

    rhs = [jnp.concatenate([beta[u] * v[u], beta[u] * ecol[u] * k[u]], axis=-1).astype(BF16) for u in idx]
    uw = [_dot(inv[u].astype(BF16), rhs[u]) for u in idx]

    state = [s_ref[b, h] for b, h in units]
    ws = [_dot(jnp.concatenate([uw[u][:, hd:], q[u] * ecol[u]], axis=0).astype(BF16), state[u].astype(BF16))
          for u in idx]
    v_new = [(uw[u][:, :hd] - ws[u][:cs]).astype(BF16) for u in idx]
    o = [ws[u][cs:] + _dot(qk[u].astype(BF16), v_new[u]) for u in idx]
    for u, (b, h) in enumerate(units):
        kdec = k[u] * jnp.exp(glast[u] - gcol[u])
        s_ref[b, h] = state[u] * jnp.exp(glast[u]) + _dot_tn(kdec.astype(BF16), v_new[u])
    for u, (b, h) in enumerate(units):
        oh = o[u] * lax.rsqrt(jnp.mean(o[u] * o[u], axis=-1, keepdims=True) + NORM_EPS) * ng_ref[...]
        oh = oh * _silu(z_ref[b, :, h * hd:(h + 1) * hd])
        o_ref[b, :, h * hd:(h + 1) * hd] = oh.astype(o_ref.dtype)
    ext_ref[:, 0:SUBLANES, :] = ext_ref[:, cs:cs + SUBLANES, :]


def _deltanet(main3, ba3, conv_buf8, conv_w, gpar, dn_norm, s0, *, cs, nbk):
    n, t, _ = main3.shape
    nc = t // cs
    hd = DN_HEAD_DIM
    return pl.pallas_call(
        functools.partial(_deltanet_kernel, cs=cs, nbk=nbk),
        grid=(n // nbk, nc),
        in_specs=[
            pl.BlockSpec((nbk, cs, DN_CONV_DIM), lambda b, c: (b, c, COL_QKV // DN_CONV_DIM)),
            pl.BlockSpec((nbk, cs, DN_WIDTH), lambda b, c: (b, c, COL_ZDN // DN_WIDTH)),
            pl.BlockSpec((nbk, cs, BA_DIM), lambda b, c: (b, c, 0)),
            pl.BlockSpec((nbk, SUBLANES, DN_CONV_DIM), lambda b, c: (b, 0, 0)),
            pl.BlockSpec((CONV_W, DN_CONV_DIM), lambda b, c: (0, 0)),
            pl.BlockSpec((2, BA_DIM), lambda b, c: (0, 0)),
            pl.BlockSpec((1, hd), lambda b, c: (0, 0)),
            pl.BlockSpec((nbk, DN_HEADS, hd, hd), lambda b, c: (b, 0, 0, 0)),
        ],
        out_specs=[
            pl.BlockSpec((nbk, cs, DN_WIDTH), lambda b, c: (b, c, 0)),
            pl.BlockSpec((nbk, DN_HEADS, hd, hd), lambda b, c: (b, 0, 0, 0)),
        ],
        out_shape=[
            jax.ShapeDtypeStruct((n, t, DN_WIDTH), BF16),
            jax.ShapeDtypeStruct((n, DN_HEADS, hd, hd), F32),
        ],
        scratch_shapes=[
            pltpu.VMEM((nbk, SUBLANES + cs, DN_CONV_DIM), F32),
            pltpu.VMEM((nbk, LANES, cs), F32),
        ],
        compiler_params=_cparams(("arbitrary", "arbitrary")),
        name="deltanet",
    )(main3, main3, ba3, conv_buf8, conv_w, gpar, dn_norm.reshape(1, hd), s0)


def _rope(x, cos, sin_signed):
    lane = lax.broadcasted_iota(jnp.int32, x.shape, 1)
    half = ATT_HEAD_DIM // 2
    swapped = jnp.where((lane % ATT_HEAD_DIM) < half,
                        pltpu.roll(x, LANES - half, axis=1), pltpu.roll(x, half, axis=1))
    return x * cos + swapped * sin_signed


def _split_heads(tile):
    lane = lax.broadcasted_iota(jnp.int32, tile.shape, 1)
    lo = lane < ATT_HEAD_DIM
    rolled = pltpu.roll(tile, ATT_HEAD_DIM, axis=1)
    zero = jnp.zeros_like(tile)
    return (jnp.where(lo, tile, zero), jnp.where(lo, zero, rolled),
            jnp.where(lo, rolled, zero), jnp.where(lo, zero, tile))


def _store_keys(dst_ref, tile_idx, row0, rows, tile):
    a_top, a_bot, b_top, b_bot = _split_heads(tile)
    ja = PAIR * tile_idx
    dst_ref[ja, pl.ds(row0, rows), :] = a_top.astype(BF16)
    dst_ref[ja, pl.ds(KEYS_PAD + row0, rows), :] = a_bot.astype(BF16)
    dst_ref[ja + 1, pl.ds(row0, rows), :] = b_top.astype(BF16)
    dst_ref[ja + 1, pl.ds(KEYS_PAD + row0, rows), :] = b_bot.astype(BF16)


def _attend_pairs(q_ref, z_ref, cos, sin_signed, sinks_ref, k2_ref, v2_ref, o_ref, valid, group=8):
    rows = q_ref.shape[0]
    lane = lax.broadcasted_iota(jnp.int32, (rows, LANES), 1)
    lo = lane < ATT_HEAD_DIM
    neg_inf = jnp.full((rows, PAIR * KEYS_PAD), -jnp.inf, F32)

    def kv_of(p):
        return p // (N_PAIRS // ATT_KV_HEADS)

    def scores(p):
        cols = slice(p * LANES, (p + 1) * LANES)
        q = _rope(q_ref[:, cols], cos, sin_signed) * ATT_HEAD_DIM ** -0.5
        return _dot_nt(q.astype(BF16), k2_ref[kv_of(p)])

    def softmax(pairs, s_list):
        heads = [(p, hh) for p in pairs for hh in range(PAIR)]
        sh = [jnp.where(valid, s, neg_inf)[:, hh * KEYS_PAD:(hh + 1) * KEYS_PAD]
              for s in s_list for hh in range(PAIR)]
        sink = [sinks_ref[PAIR * p + hh] for p, hh in heads]
        m = [jnp.maximum(jnp.max(x, axis=-1, keepdims=True), sk) for x, sk in zip(sh, sink)]
        ph = [jnp.exp(x - mx) for x, mx in zip(sh, m)]
        probs = [jnp.concatenate(ph[PAIR * i:PAIR * (i + 1)], axis=-1).astype(BF16) for i in range(len(pairs))]
        inv = [1.0 / (jnp.sum(x, axis=-1, keepdims=True) + jnp.exp(sk - mx)) for x, sk, mx in zip(ph, sink, m)]
        inv_den = [jnp.where(lo, inv[PAIR * i], inv[PAIR * i + 1]) for i in range(len(pairs))]
        return probs, inv_den

    def finish(p, probs, inv_den):
        cols = slice(p * LANES, (p + 1) * LANES)
        o = _dot(probs, v2_ref[kv_of(p)]) * inv_den
        o_ref[:, cols] = (o * _silu(z_ref[:, cols])).astype(o_ref.dtype)

    groups = [range(g, g + group) for g in range(0, N_PAIRS, group)]
    s_next = [scores(p) for p in groups[0]]
    for gi, pairs in enumerate(groups):
        s_cur = s_next
        if gi + 1 < len(groups):
            s_next = [scores(p) for p in groups[gi + 1]]
        probs, inv_den = softmax(pairs, s_cur)
        for i, p in enumerate(pairs):
            finish(p, probs[i], inv_den[i])


def _swa_prompt_kernel(sinks_ref, q_ref, z_ref, k_ref, v_ref, cos_ref, sin_ref,
                       o_ref, kout_ref, k2_ref, v2_ref, *, n_steps):
    s = pl.program_id(1)
    rows = STEP_CHUNKS * CHUNK

    @pl.when(s == 0)
    def _():
        k2_ref[...] = jnp.zeros_like(k2_ref)
        v2_ref[...] = jnp.zeros_like(v2_ref)

    cos = cos_ref[...]
    sin_signed = sin_ref[...]
    half = s % (KEY_SLOTS // STEP_CHUNKS)
    row0 = pl.multiple_of(half * rows, rows)
    for ti in range(ATT_KV_WIDTH // LANES):
        k_rot = _rope(k_ref[:, ti * LANES:(ti + 1) * LANES], cos, sin_signed)
        _store_keys(k2_ref, ti, row0, rows, k_rot)
        _store_keys(v2_ref, ti, row0, rows, v_ref[:, ti * LANES:(ti + 1) * LANES])

        @pl.when(s == n_steps - 1)
        def _():
            kout_ref[0, :, ti * LANES:(ti + 1) * LANES] = k_rot

    shape = (rows, PAIR * KEYS_PAD)
    key_slot = (lax.broadcasted_iota(jnp.int32, shape, 1) % KEYS_PAD) // CHUNK
    rel = key_slot % STEP_CHUNKS - jnp.where(key_slot // STEP_CHUNKS == half, 0, STEP_CHUNKS)
    q_chunk = lax.broadcasted_iota(jnp.int32, shape, 0) // CHUNK
    valid = (rel <= q_chunk) & (rel >= q_chunk - WINDOW // CHUNK) & (s * STEP_CHUNKS + rel >= 0)
    _attend_pairs(q_ref, z_ref, cos, sin_signed, sinks_ref, k2_ref, v2_ref, o_ref, valid)


def _swa_prompt(main, sinks, cos_t, sin_t, *, n, t):
    rows = STEP_CHUNKS * CHUNK
    assert t % rows == 0 and rows == WINDOW
    ns = t // rows
    kvw = ATT_KV_WIDTH
    grid_spec = pltpu.PrefetchScalarGridSpec(
        num_scalar_prefetch=1,
        grid=(n, ns),
        in_specs=[
            pl.BlockSpec((rows, ATT_WIDTH), lambda b, c, s: (b * ns + c, COL_QAT // ATT_WIDTH)),
            pl.BlockSpec((rows, ATT_WIDTH), lambda b, c, s: (b * ns + c, COL_ZAT // ATT_WIDTH)),
            pl.BlockSpec((rows, kvw), lambda b, c, s: (b * ns + c, COL_KAT // kvw)),
            pl.BlockSpec((rows, kvw), lambda b, c, s: (b * ns + c, COL_VAT // kvw)),
            pl.BlockSpec((rows, LANES), lambda b, c, s: (c, 0)),
            pl.BlockSpec((rows, LANES), lambda b, c, s: (c, 0)),
        ],
        out_specs=[
            pl.BlockSpec((rows, ATT_WIDTH), lambda b, c, s: (b * ns + c, 0)),
            pl.BlockSpec((1, WINDOW, kvw), lambda b, c, s: (b, 0, 0)),
        ],
        scratch_shapes=[
            pltpu.VMEM((ATT_KV_HEADS, PAIR * KEYS_PAD, LANES), BF16),
            pltpu.VMEM((ATT_KV_HEADS, PAIR * KEYS_PAD, LANES), BF16),
        ],
    )
    return pl.pallas_call(
        functools.partial(_swa_prompt_kernel, n_steps=ns),
        grid_spec=grid_spec,
        out_shape=[
            jax.ShapeDtypeStruct((n * t, ATT_WIDTH), BF16),
            jax.ShapeDtypeStruct((n, WINDOW, kvw), F32),
        ],
        compiler_params=_cparams(("arbitrary", "arbitrary")),
        name="swa_prompt",
    )(sinks, main, main, main, main, cos_t, sin_t)


def _swa_sample_kernel(sinks_ref, q_ref, z_ref, k_ref, v_ref, ck_ref, cv_ref, cos_ref, sin_ref,
                       o_ref, kout_ref, k2_ref, v2_ref, *, t, past):
    k2_ref[...] = jnp.zeros_like(k2_ref)
    v2_ref[...] = jnp.zeros_like(v2_ref)
    cos = cos_ref[...]
    sin_signed = sin_ref[...]
    for ti in range(ATT_KV_WIDTH // LANES):
        cols = slice(ti * LANES, (ti + 1) * LANES)
        k_rot = _rope(k_ref[:, cols], cos, sin_signed)
        _store_keys(k2_ref, ti, 0, past, ck_ref[0, :, cols])
        _store_keys(k2_ref, ti, past, t, k_rot)
        _store_keys(v2_ref, ti, 0, past, cv_ref[0, :, cols])
        _store_keys(v2_ref, ti, past, t, v_ref[:, cols])
        kout_ref[0, 0:past - t, cols] = ck_ref[0, t:past, cols]
        kout_ref[0, past - t:past, cols] = k_rot
    key_pos = lax.broadcasted_iota(jnp.int32, (t, PAIR * KEYS_PAD), 1) % KEYS_PAD
    valid = key_pos < past + t
    _attend_pairs(q_ref, z_ref, cos, sin_signed, sinks_ref, k2_ref, v2_ref, o_ref, valid)


def _swa_sample(main, sinks, cache_k, cache_v, cos_t, sin_t, *, n, t):
    past = cache_k.shape[1]
    assert past + t <= KEYS_PAD and t <= past
    kvw = ATT_KV_WIDTH
    grid_spec = pltpu.PrefetchScalarGridSpec(
        num_scalar_prefetch=1,
        grid=(n,),
        in_specs=[
            pl.BlockSpec((t, ATT_WIDTH), lambda b, s: (b, COL_QAT // ATT_WIDTH)),
            pl.BlockSpec((t, ATT_WIDTH), lambda b, s: (b, COL_ZAT // ATT_WIDTH)),
            pl.BlockSpec((t, kvw), lambda b, s: (b, COL_KAT // kvw)),
            pl.BlockSpec((t, kvw), lambda b, s: (b, COL_VAT // kvw)),
            pl.BlockSpec((1, past, kvw), lambda b, s: (b, 0, 0)),
            pl.BlockSpec((1, past, kvw), lambda b, s: (b, 0, 0)),
            pl.BlockSpec((t, LANES), lambda b, s: (0, 0)),
            pl.BlockSpec((t, LANES), lambda b, s: (0, 0)),
        ],
        out_specs=[
            pl.BlockSpec((t, ATT_WIDTH), lambda b, s: (b, 0)),
            pl.BlockSpec((1, past, kvw), lambda b, s: (b, 0, 0)),
        ],
        scratch_shapes=[
            pltpu.VMEM((ATT_KV_HEADS, PAIR * KEYS_PAD, LANES), BF16),
            pltpu.VMEM((ATT_KV_HEADS, PAIR * KEYS_PAD, LANES), BF16),
        ],
    )
    return pl.pallas_call(
        functools.partial(_swa_sample_kernel, t=t, past=past),
        grid_spec=grid_spec,
        out_shape=[
            jax.ShapeDtypeStruct((n * t, ATT_WIDTH), BF16),
            jax.ShapeDtypeStruct((n, past, kvw), F32),
        ],
        compiler_params=_cparams(("arbitrary",)),
        name="swa_sample",
    )(sinks, main, main, main, main, cache_k, cache_v, cos_t, sin_t)


def _out_proj_kernel(odn_ref, oat_ref, w_ref, x_ref, mod_ref, gpost_ref, *rest, tt, nk, emit_h):
    if emit_h:
        modn_ref, gpre_ref, xo_ref, h_ref = rest
    else:
        (xo_ref,) = rest
    kk = pl.program_id(1)
    half = nk // 2

    @pl.when(kk == 0)
    def _():
        xo_ref[...] = jnp.zeros_like(xo_ref)

    @pl.when(kk < half)
    def _():
        xo_ref[...] += _dot(odn_ref[...], w_ref[...])

    @pl.when(kk >= half)
    def _():
        xo_ref[...] += _dot(oat_ref[...], w_ref[...])

    @pl.when(kk == nk - 1)
    def _():
        def body(chunks):
            ys = [xo_ref[rows, :] for rows, _ in chunks]
            x_new = [x_ref[rows, :] + mod_ref[b, 2] * (y * r * gpost_ref[...])
                     for (rows, b), y, r in zip(chunks, ys, _inv_rms(ys))]
            for (rows, _), xn in zip(chunks, x_new):
                xo_ref[rows, :] = xn
            if emit_h:
                for (rows, b), xn, r in zip(chunks, x_new, _inv_rms(x_new)):
                    h_ref[rows, :] = (xn * r * gpre_ref[...] * (1.0 + modn_ref[b, 1])
                                      + modn_ref[b, 0]).astype(BF16)

        _row_chunks(xo_ref.shape[0], tt, body)


def _out_proj(o_dn, o_at, w_out_b, layer, x2, mod, g_post, mod_next, g_pre_next, *, nb, tt, tk):
    m, d = x2.shape
    tm = nb * tt
    nk = (DN_WIDTH + ATT_WIDTH) // tk
    half = nk // 2
    tiles_per_batch = (m // mod.shape[0]) // tt
    emit_h = mod_next is not None
    mod_spec = pl.BlockSpec((nb, 3, ROW_CHUNK, d), lambda i, k: (i // tiles_per_batch, 0, 0, 0))
    vec_spec = pl.BlockSpec((ROW_CHUNK, d), lambda i, k: (0, 0))
    row_spec = pl.BlockSpec((tm, d), lambda i, k: (i, 0))
    in_specs = [
        pl.BlockSpec((tm, tk), lambda i, k: (i, jnp.minimum(k, half - 1))),
        pl.BlockSpec((tm, tk), lambda i, k: (i, jnp.maximum(k - half, 0))),
        pl.BlockSpec((None, tk, d), lambda i, k: (layer, k, 0)),
        row_spec, mod_spec, vec_spec,
    ]
    args = [o_dn, o_at, w_out_b, x2, mod, _rows_bcast(g_post)]
    out_specs = [row_spec]
    out_shape = [jax.ShapeDtypeStruct((m, d), F32)]
    if emit_h:
        in_specs += [mod_spec, vec_spec]
        args += [mod_next, _rows_bcast(g_pre_next)]
        out_specs.append(row_spec)
        out_shape.append(jax.ShapeDtypeStruct((m, d), BF16))
    outs = pl.pallas_call(
        functools.partial(_out_proj_kernel, tt=tt, nk=nk, emit_h=emit_h),
        grid=(m // tm, nk),
        in_specs=in_specs,
        out_specs=out_specs,
        out_shape=out_shape,
        compiler_params=_cparams(("arbitrary", "arbitrary")),
        name="out_proj",
    )(*args)
    return (outs[0], outs[1]) if emit_h else (outs[0], None)


def _rope_tables(pos):
    half = ATT_HEAD_DIM // 2
    inv = ROPE_THETA ** (-jnp.arange(half, dtype=F32) / half)
    ang = pos.astype(F32)[:, None] * inv[None, :]
    cos = jnp.cos(ang)
    sin = jnp.sin(ang)
    reps = LANES // ATT_HEAD_DIM
    return jnp.tile(jnp.concatenate([cos, cos], axis=-1), (1, reps)), \
        jnp.tile(jnp.concatenate([-sin, sin], axis=-1), (1, reps))


def _layer(x2, h, mod, mod_next, conv_buf, s0, cache_k, cache_v, pos, lw, layer, *, n, t, prompt):
    (g_post, g_pre_next, w_dn, w_at, w_ba, conv_w, gpar, dn_norm, sinks, w_out_b) = lw
    if prompt:
        nb, tt, tm_in, cs = 1, 512, min(1024, n * t), CHUNK
    else:
        nb, tt, tm_in, cs = n, t, n * t, min(CHUNK, t)
    (main_dn,) = _in_proj(h, w_dn, layer, tm=tm_in, tn=1024)
    main, ba = _in_proj(h, w_at, layer, tm=tm_in, tn=768, w_ba=w_ba)

    conv_buf8 = jnp.pad(conv_buf, ((0, 0), (SUBLANES - (CONV_W - 1), 0), (0, 0)))
    main_dn3 = main_dn.reshape(n, t, DN_PART_DIM)
    o_dn, s_new = _deltanet(main_dn3, ba.reshape(n, t, BA_DIM), conv_buf8, conv_w, gpar, dn_norm, s0,
                            cs=cs, nbk=DN_SEQS_PER_STEP)
    o_dn = o_dn.reshape(n * t, DN_WIDTH)
    conv_new = main_dn3[:, t - (CONV_W - 1):, COL_QKV:COL_QKV + DN_CONV_DIM]

    cos_t, sin_t = _rope_tables(pos)
    v_rows = main.reshape(n, t, AT_PART_DIM)[:, :, COL_VAT:COL_VAT + ATT_KV_WIDTH]
    if prompt:
        o_at, k_new = _swa_prompt(main, sinks, cos_t, sin_t, n=n, t=t)
        v_new = v_rows[:, t - WINDOW:]
    else:
        past = cache_k.shape[1]
        ck = cache_k.reshape(n, past, ATT_KV_WIDTH)
        cv = cache_v.reshape(n, past, ATT_KV_WIDTH)
        o_at, k_new = _swa_sample(main, sinks, ck, cv, cos_t, sin_t, n=n, t=t)
        v_new = jnp.concatenate([cv, v_rows], axis=1)[:, -past:]
    rows = k_new.shape[1]
    k_new = k_new.reshape(n, rows, ATT_KV_HEADS, ATT_HEAD_DIM)
    v_new = v_new.reshape(n, rows, ATT_KV_HEADS, ATT_HEAD_DIM)

    x_new, h_next = _out_proj(o_dn, o_at, w_out_b, layer, x2, mod, g_post, mod_next, g_pre_next,
                              nb=nb, tt=tt, tk=512)
    return x_new, h_next, conv_new, s_new, k_new, v_new


def kernel(x_prompt, x_sample, state_conv, state_dn, cache_k, cache_v, c_prompt, c_sample,
           w_ada, b_ada, g_pre, g_post, w_in, conv_w, a_log, dt_bias, dn_norm, sinks, w_out):
    depth = w_in.shape[0]
    bp, tp, d = x_prompt.shape
    bs, ts, _ = x_sample.shape
    pos_p = jnp.arange(tp)
    pos_s = PAST_LEN + jnp.arange(ts)

    w_dn, w_at, w_ba = _w_prep(w_in)
    w_out_b = w_out.astype(BF16)
    pad_rows = (-(bp + bs)) % (2 * SUBLANES)
    c_all = jnp.concatenate([c_prompt, c_sample, jnp.zeros((pad_rows, d), F32)], axis=0)
    gate_pad = jnp.zeros((depth, BA_DIM - 2 * DN_HEADS), F32)
    head_pad = jnp.zeros((depth, DN_HEADS), F32)
    gpar = jnp.stack([jnp.concatenate([head_pad, a_log, gate_pad], axis=-1),
                      jnp.concatenate([head_pad, dt_bias, gate_pad], axis=-1)], axis=1)

    zero_conv = jnp.zeros((bp, CONV_W - 1, DN_CONV_DIM), F32)
    zero_state = jnp.zeros((bp, DN_HEADS, DN_HEAD_DIM, DN_HEAD_DIM), F32)

    ada = _ada(c_all, w_ada, b_ada)
    mod_p = _rows_bcast(ada[:, :bp].reshape(depth, bp, 3, d))
    mod_s = _rows_bcast(ada[:, bp:bp + bs].reshape(depth, bs, 3, d))

    xp = x_prompt.reshape(bp * tp, d)
    xs = x_sample.reshape(bs * ts, d)
    hp = _prenorm(xp, mod_p[0], g_pre[0], nb=1, tt=512)
    hs = _prenorm(xs, mod_s[0], g_pre[0], nb=bs, tt=ts)
    outs_p, outs_s = [], []
    for l in range(depth):
        last = l == depth - 1
        lw = (g_post[l], None if last else g_pre[l + 1], w_dn, w_at, w_ba, conv_w[l], gpar[l], dn_norm[l],
              sinks[l], w_out_b)
        xp, hp, *rest_p = _layer(xp, hp, mod_p[l], None if last else mod_p[l + 1], zero_conv, zero_state,
                                 None, None, pos_p, lw, l, n=bp, t=tp, prompt=True)
        xs, hs, *rest_s = _layer(xs, hs, mod_s[l], None if last else mod_s[l + 1], state_conv[l], state_dn[l],
                                 cache_k[l], cache_v[l], pos_s, lw, l, n=bs, t=ts, prompt=False)
        outs_p.append(rest_p)
        outs_s.append(rest_s)
    xp = xp.reshape(bp, tp, d)
    xs = xs.reshape(bs, ts, d)

    def stack(outs, i):
        return jnp.stack([o[i] for o in outs])

    return (xp, xs,
            stack(outs_p, 0), stack(outs_p, 1), stack(outs_p, 2), stack(outs_p, 3),
            stack(outs_s, 0), stack(outs_s, 1), stack(outs_s, 2), stack(outs_s, 3))
```

```python
import functools
import math

import jax
import jax.numpy as jnp
from jax import lax
from jax.experimental import pallas as pl
from jax.experimental.pallas import tpu as pltpu

F32 = jnp.float32
BF16 = jnp.bfloat16

D_MODEL = 4096
CHUNK = 64
DN_WIDTH = D_MODEL // 2
DN_HEAD_DIM = 128
DN_HEADS = DN_WIDTH // DN_HEAD_DIM
CONV_W = 4
DN_CONV_DIM = 3 * DN_WIDTH
ATT_WIDTH = D_MODEL - DN_WIDTH
ATT_HEAD_DIM = 64
ATT_Q_HEADS = ATT_WIDTH // ATT_HEAD_DIM
ATT_KV_HEADS = ATT_Q_HEADS // 8
ATT_KV_WIDTH = ATT_KV_HEADS * ATT_HEAD_DIM
WINDOW = 128
ROPE_THETA = 10000.0
NORM_EPS = 1e-6
PAST_LEN = 2048

LANES = 128
SUBLANES = 8
VMEM_LIMIT_BYTES = 56 * 1024 * 1024

COL_QKV = 0
COL_ZDN = COL_QKV + DN_CONV_DIM
DN_PART_DIM = COL_ZDN + DN_WIDTH
COL_QAT = 0
COL_ZAT = COL_QAT + ATT_WIDTH
COL_KAT = COL_ZAT + ATT_WIDTH
COL_VAT = COL_KAT + ATT_KV_WIDTH
AT_PART_DIM = COL_VAT + ATT_KV_WIDTH
BA_DIM = LANES
SRC_BETA = DN_PART_DIM
SRC_DECAY = SRC_BETA + DN_HEADS
SRC_QAT = SRC_DECAY + DN_HEADS
SRC_KAT = SRC_QAT + ATT_WIDTH
SRC_VAT = SRC_KAT + ATT_KV_WIDTH
SRC_ZAT = SRC_VAT + ATT_KV_WIDTH
IN_DIM = SRC_ZAT + ATT_WIDTH

ROW_CHUNK = 8
ROW_GROUP = 8
PAIR = LANES // ATT_HEAD_DIM
N_PAIRS = ATT_Q_HEADS // PAIR
KEY_SLOTS = 4
KEYS_PAD = KEY_SLOTS * CHUNK
STEP_CHUNKS = 2
DN_SEQS_PER_STEP = 2


def _cparams(sem):
    return pltpu.CompilerParams(dimension_semantics=sem, vmem_limit_bytes=VMEM_LIMIT_BYTES)


def _dot(a, b):
    return jnp.dot(a, b, preferred_element_type=F32)


def _dot_nt(a, b):
    return lax.dot_general(a, b, (((1,), (1,)), ((), ())), preferred_element_type=F32)


def _dot_tn(a, b):
    return lax.dot_general(a, b, (((0,), (0,)), ((), ())), preferred_element_type=F32)


def _split(a):
    hi = a.astype(BF16)
    lo = (a - hi.astype(F32)).astype(BF16)
    return hi, lo


def _dot_split(a, b):
    ah, al = _split(a)
    bh, bl = _split(b)
    return _dot(ah, bh) + (_dot(ah, bl) + _dot(al, bh))


def _sigmoid(x):
    return 1.0 / (1.0 + jnp.exp(-x))


def _silu(x):
    return x * _sigmoid(x)


def _ada_kernel(c_ref, w_ref, b_ref, o_ref):
    s = _silu(c_ref[...]).astype(BF16)
    o_ref[...] = _dot(s, w_ref[...].astype(BF16)) + b_ref[...]


def _ada(c_all, w_ada, b_ada):
    rows, d = c_all.shape
    depth, _, n_out = w_ada.shape
    tn = 512
    return pl.pallas_call(
        _ada_kernel,
        grid=(depth, n_out // tn),
        in_specs=[
            pl.BlockSpec((rows, d), lambda l, j: (0, 0)),
            pl.BlockSpec((None, d, tn), lambda l, j: (l, 0, j)),
            pl.BlockSpec((None, 1, tn), lambda l, j: (l, 0, j)),
        ],
        out_specs=pl.BlockSpec((None, rows, tn), lambda l, j: (l, 0, j)),
        out_shape=jax.ShapeDtypeStruct((depth, rows, n_out), F32),
        compiler_params=_cparams(("arbitrary", "arbitrary")),
        name="ada",
    )(c_all, w_ada, b_ada.reshape(depth, 1, n_out))


def _rows_bcast(v):
    return jnp.broadcast_to(v[..., None, :], v.shape[:-1] + (ROW_CHUNK, v.shape[-1]))


def _row_chunks(n_rows, rows_per_batch, body):
    chunks_per_batch = rows_per_batch // ROW_CHUNK
    group = math.gcd(ROW_GROUP, n_rows // ROW_CHUNK)

    def step(r, carry):
        chunks = []
        for u in range(group):
            ci = r * group + u
            chunks.append((pl.ds(pl.multiple_of(ci * ROW_CHUNK, ROW_CHUNK), ROW_CHUNK), ci // chunks_per_batch))
        body(chunks)
        return carry

    lax.fori_loop(0, n_rows // (ROW_CHUNK * group), step, 0)


def _inv_rms(xs):
    return [lax.rsqrt(jnp.mean(x * x, axis=-1, keepdims=True) + NORM_EPS) for x in xs]


def _prenorm_kernel(x_ref, mod_ref, g_ref, h_ref, *, tt):
    def body(chunks):
        xs = [x_ref[rows, :] for rows, _ in chunks]
        for (rows, b), x, r in zip(chunks, xs, _inv_rms(xs)):
            h_ref[rows, :] = (x * r * g_ref[...] * (1.0 + mod_ref[b, 1]) + mod_ref[b, 0]).astype(BF16)

    _row_chunks(x_ref.shape[0], tt, body)


def _prenorm(x2, mod, g_pre, *, nb, tt):
    m, d = x2.shape
    tm = nb * tt
    tiles_per_batch = (m // mod.shape[0]) // tt
    return pl.pallas_call(
        functools.partial(_prenorm_kernel, tt=tt),
        grid=(m // tm,),
        in_specs=[
            pl.BlockSpec((tm, d), lambda i: (i, 0)),
            pl.BlockSpec((nb, 3, ROW_CHUNK, d), lambda i: (i // tiles_per_batch, 0, 0, 0)),
            pl.BlockSpec((ROW_CHUNK, d), lambda i: (0, 0)),
        ],
        out_specs=pl.BlockSpec((tm, d), lambda i: (i, 0)),
        out_shape=jax.ShapeDtypeStruct((m, d), BF16),
        compiler_params=_cparams(("arbitrary",)),
        name="prenorm",
    )(x2, mod, _rows_bcast(g_pre))


def _cast_kernel(starts_ref, w_ref, o_ref, *, valid_rows):
    w = w_ref[0]
    if valid_rows < w.shape[0]:
        row = lax.broadcasted_iota(jnp.int32, w.shape, 0)
        w = jnp.where(row < valid_rows, w, 0.0)
    o_ref[...] = w.astype(BF16)


def _cast_rows(w_t, starts, tr, valid_rows=None):
    depth, _, d = w_t.shape
    n_blk = len(starts)
    grid_spec = pltpu.PrefetchScalarGridSpec(
        num_scalar_prefetch=1,
        grid=(depth, n_blk),
        in_specs=[pl.BlockSpec((pl.Element(1), pl.Element(tr), pl.Element(d)),
                               lambda l, i, st: (l, pl.multiple_of(st[i], 2 * SUBLANES), 0))],
        out_specs=pl.BlockSpec((None, tr, d), lambda l, i, st: (l, i, 0)),
    )
    return pl.pallas_call(
        functools.partial(_cast_kernel, valid_rows=tr if valid_rows is None else valid_rows),
        grid_spec=grid_spec,
        out_shape=jax.ShapeDtypeStruct((depth, n_blk * tr, d), BF16),
        compiler_params=_cparams(("arbitrary", "arbitrary")),
        name="w_prep",
    )(jnp.asarray(starts, jnp.int32), w_t)


def _w_prep(w_in):
    assert w_in.shape[-1] == IN_DIM
    w_t = jnp.swapaxes(w_in, 1, 2)
    tr = ATT_KV_WIDTH
    w_dn = _cast_rows(w_t, list(range(0, DN_PART_DIM, tr)), tr)
    at_rows = ([SRC_QAT + r for r in range(0, ATT_WIDTH, tr)] + [SRC_ZAT + r for r in range(0, ATT_WIDTH, tr)]
               + [SRC_KAT, SRC_VAT])
    w_at = _cast_rows(w_t, at_rows, tr)
    w_ba = _cast_rows(w_t, [SRC_BETA], BA_DIM, valid_rows=2 * DN_HEADS)
    return w_dn, w_at, w_ba


def _in_proj_kernel(h_ref, w_ref, *rest):
    if len(rest) == 3:
        wba_ref, o_ref, oba_ref = rest

        @pl.when(pl.program_id(1) == 0)
        def _():
            oba_ref[...] = _dot_nt(h_ref[...], wba_ref[...])
    else:
        (o_ref,) = rest
    o_ref[...] = _dot_nt(h_ref[...], w_ref[...])


def _in_proj(h, w, layer, *, tm, tn, w_ba=None):
    m, d = h.shape
    n_out = w.shape[1]
    in_specs = [
        pl.BlockSpec((tm, d), lambda i, j: (i, 0)),
        pl.BlockSpec((None, tn, d), lambda i, j: (layer, j, 0)),
    ]
    out_specs = [pl.BlockSpec((tm, tn), lambda i, j: (i, j))]
    out_shape = [jax.ShapeDtypeStruct((m, n_out), F32)]
    args = [h, w]
    if w_ba is not None:
        in_specs.append(pl.BlockSpec((None, BA_DIM, d), lambda i, j: (layer, 0, 0)))
        out_specs.append(pl.BlockSpec((tm, BA_DIM), lambda i, j: (i, 0)))
        out_shape.append(jax.ShapeDtypeStruct((m, BA_DIM), F32))
        args.append(w_ba)
    return pl.pallas_call(
        _in_proj_kernel,
        grid=(m // tm, n_out // tn),
        in_specs=in_specs,
        out_specs=out_specs,
        out_shape=out_shape,
        compiler_params=_cparams(("arbitrary", "arbitrary")),
        name="in_proj",
    )(*args)


def _deltanet_kernel(qkv_ref, qkvn_ref, z_ref, ba_ref, cbuf_ref, cw_ref, gpar_ref, ng_ref, s0_ref,
                     o_ref, s_ref, ext_ref, gct_ref, q_scr, k_scr, v_scr, *, cs, nbk):
    c = pl.program_id(1)
    hd = DN_HEAD_DIM
    units = [(b, h) for b in range(nbk) for h in range(DN_HEADS)]
    idx = range(len(units))
    hist = SUBLANES - (CONV_W - 1)

    def conv_silu(b, col0):
        cols = slice(col0, col0 + hd)
        acc = ext_ref[b, hist:hist + cs, cols] * cw_ref[0:1, cols]
        for j in range(1, CONV_W):
            acc = acc + ext_ref[b, hist + j:hist + j + cs, cols] * cw_ref[j:j + 1, cols]
        return _silu(acc)

    def front_end(slot, part=0, parts=1):
        per = -(-len(units) // parts)
        for u, (b, h) in list(enumerate(units))[part * per:(part + 1) * per]:
            qh = conv_silu(b, h * hd)
            kh = conv_silu(b, DN_WIDTH + h * hd)
            q_scr[slot, u] = qh * lax.rsqrt(jnp.sum(qh * qh, axis=-1, keepdims=True) + NORM_EPS) * hd ** -0.5
            k_scr[slot, u] = kh * lax.rsqrt(jnp.sum(kh * kh, axis=-1, keepdims=True) + NORM_EPS)
            v_scr[slot, u] = conv_silu(b, 2 * DN_WIDTH + h * hd)

    @pl.when(c == 0)
    def _():
        s_ref[...] = s0_ref[...]
        ext_ref[:, 0:SUBLANES, :] = cbuf_ref[...]
        ext_ref[:, SUBLANES:SUBLANES + cs, :] = qkv_ref[...]
        front_end(0)

    cur = c % 2

    row = lax.broadcasted_iota(jnp.int32, (cs, LANES), 0)
    lane = lax.broadcasted_iota(jnp.int32, (cs, LANES), 1)
    beta_all, gc_all = [], []
    for b in range(nbk):
        ba = ba_ref[b]
        xg = ba + gpar_ref[1:2, :]
        softplus = jnp.maximum(xg, 0.0) + jnp.log(1.0 + jnp.exp(-jnp.abs(xg)))
        gc = -jnp.exp(gpar_ref[0:1, :]) * softplus
        shift = 1
        while shift < cs:
            gc = gc + jnp.where(row >= shift, pltpu.roll(gc, shift, axis=0), 0.0)
            shift *= 2
        gct_ref[b] = gc.T
        beta_all.append(_sigmoid(ba))
        gc_all.append(gc)

    ri = lax.broadcasted_iota(jnp.int32, (cs, cs), 0)
    ci = lax.broadcasted_iota(jnp.int32, (cs, cs), 1)
    causal = ri >= ci
    strict = ri > ci
    eye = jnp.where(ri == ci, 1.0, 0.0).astype(F32)
    n_doublings = int(math.log2(cs)) - 1

    q = [q_scr[cur, u] for u in idx]
    k = [k_scr[cur, u] for u in idx]
    v = [v_scr[cur, u] for u in idx]

    beta = [jnp.sum(jnp.where(lane == h, beta_all[b], 0.0), axis=-1, keepdims=True) for b, h in units]
    gcol = [jnp.sum(jnp.where(lane == DN_HEADS + h, gc_all[b], 0.0), axis=-1, keepdims=True) for b, h in units]
    grow = [gct_ref[b, DN_HEADS + h:DN_HEADS + h + 1, :] for b, h in units]
    glast = [g[:, cs - 1:cs] for g in grow]
    gamma = [jnp.where(causal, jnp.exp(gcol[u] - grow[u]), 0.0) for u in idx]
    ecol = [jnp.exp(g) for g in gcol]

    kb = [x.astype(BF16) for x in k]
    kq = [_dot_nt(jnp.concatenate([kb[u], q[u].astype(BF16)], axis=0), kb[u]) for u in idx]

    ext_ref[:, 0:SUBLANES, :] = qkv_ref[:, cs - SUBLANES:cs, :]
    ext_ref[:, SUBLANES:SUBLANES + cs, :] = qkvn_ref[...]
    parts = n_doublings + 3
    front_end(1 - cur, 0, parts)

    a_mat = [jnp.where(strict, beta[u] * kq[u][:cs] * gamma[u], 0.0) for u in idx]
    qk = [kq[u][cs:] * gamma[u] for u in idx]

    inv = [eye - a for a in a_mat]
    apow = [a.astype(BF16) for a in a_mat]
    for it in range(n_doublings):
        apow = [_dot(a, a).astype(BF16) for a in apow]
        inv = [inv[u] + _dot(inv[u].astype(BF16), apow[u]) for u in idx]
        front_end(1 - cur, 1 + it, parts)
    resid = [eye - inv[u] - _dot_split(a_mat[u], inv[u]) for u in idx]
    inv = [inv[u] + _dot(inv[u].astype(BF16), resid[u].astype(BF16)) for u in idx]
    front_end(1 - cur, n_doublings + 1, parts)

    rhs = [jnp.concatenate([beta[u] * v[u], beta[u] * ecol[u] * k[u]], axis=-1).astype(BF16) for u in idx]
    uw = [_dot(inv[u].astype(BF16), rhs[u]) for u in idx]
    front_end(1 - cur, n_doublings + 2, parts)

    state = [s_ref[b, h] for b, h in units]
    ws = [_dot(jnp.concatenate([uw[u][:, hd:], q[u] * ecol[u]], axis=0).astype(BF16), state[u].astype(BF16))
          for u in idx]
    v_new = [(uw[u][:, :hd] - ws[u][:cs]).astype(BF16) for u in idx]
    o = [ws[u][cs:] + _dot(qk[u].astype(BF16), v_new[u]) for u in idx]
    for u, (b, h) in enumerate(units):
        kdec = k[u] * jnp.exp(glast[u] - gcol[u])
        s_ref[b, h] = state[u] * jnp.exp(glast[u]) + _dot_tn(kdec.astype(BF16), v_new[u])
    for u, (b, h) in enumerate(units):
        oh = o[u] * lax.rsqrt(jnp.mean(o[u] * o[u], axis=-1, keepdims=True) + NORM_EPS) * ng_ref[...]
        oh = oh * _silu(z_ref[b, :, h * hd:(h + 1) * hd])
        o_ref[b, :, h * hd:(h + 1) * hd] = oh.astype(o_ref.dtype)


def _deltanet(main3, ba3, conv_buf8, conv_w, gpar, dn_norm, s0, *, cs, nbk):
    n, t, _ = main3.shape
    nc = t // cs
    hd = DN_HEAD_DIM
    return pl.pallas_call(
        functools.partial(_deltanet_kernel, cs=cs, nbk=nbk),
        grid=(n // nbk, nc),
        in_specs=[
            pl.BlockSpec((nbk, cs, DN_CONV_DIM), lambda b, c: (b, c, COL_QKV // DN_CONV_DIM)),
            pl.BlockSpec((nbk, cs, DN_CONV_DIM), lambda b, c: (b, jnp.minimum(c + 1, nc - 1), COL_QKV // DN_CONV_DIM)),
            pl.BlockSpec((nbk, cs, DN_WIDTH), lambda b, c: (b, c, COL_ZDN // DN_WIDTH)),
            pl.BlockSpec((nbk, cs, BA_DIM), lambda b, c: (b, c, 0)),
            pl.BlockSpec((nbk, SUBLANES, DN_CONV_DIM), lambda b, c: (b, 0, 0)),
            pl.BlockSpec((CONV_W, DN_CONV_DIM), lambda b, c: (0, 0)),
            pl.BlockSpec((2, BA_DIM), lambda b, c: (0, 0)),
            pl.BlockSpec((1, hd), lambda b, c: (0, 0)),
            pl.BlockSpec((nbk, DN_HEADS, hd, hd), lambda b, c: (b, 0, 0, 0)),
        ],
        out_specs=[
            pl.BlockSpec((nbk, cs, DN_WIDTH), lambda b, c: (b, c, 0)),
            pl.BlockSpec((nbk, DN_HEADS, hd, hd), lambda b, c: (b, 0, 0, 0)),
        ],
        out_shape=[
            jax.ShapeDtypeStruct((n, t, DN_WIDTH), BF16),
            jax.ShapeDtypeStruct((n, DN_HEADS, hd, hd), F32),
        ],
        scratch_shapes=[
            pltpu.VMEM((nbk, SUBLANES + cs, DN_CONV_DIM), F32),
            pltpu.VMEM((nbk, LANES, cs), F32),
        ] + [pltpu.VMEM((2, nbk * DN_HEADS, cs, hd), F32)] * 3,
        compiler_params=_cparams(("arbitrary", "arbitrary")),
        name="deltanet",
    )(main3, main3, main3, ba3, conv_buf8, conv_w, gpar, dn_norm.reshape(1, hd), s0)


def _rope(x, cos, sin_signed):
    lane = lax.broadcasted_iota(jnp.int32, x.shape, 1)
    half = ATT_HEAD_DIM // 2
    swapped = jnp.where((lane % ATT_HEAD_DIM) < half,
                        pltpu.roll(x, LANES - half, axis=1), pltpu.roll(x, half, axis=1))
    return x * cos + swapped * sin_signed


def _split_heads(tile):
    lane = lax.broadcasted_iota(jnp.int32, tile.shape, 1)
    lo = lane < ATT_HEAD_DIM
    rolled = pltpu.roll(tile, ATT_HEAD_DIM, axis=1)
    zero = jnp.zeros_like(tile)
    return (jnp.where(lo, tile, zero), jnp.where(lo, zero, rolled),
            jnp.where(lo, rolled, zero), jnp.where(lo, zero, tile))


def _store_keys(dst_ref, tile_idx, row0, rows, tile):
    a_top, a_bot, b_top, b_bot = _split_heads(tile)
    ja = PAIR * tile_idx
    dst_ref[ja, pl.ds(row0, rows), :] = a_top.astype(BF16)
    dst_ref[ja, pl.ds(KEYS_PAD + row0, rows), :] = a_bot.astype(BF16)
    dst_ref[ja + 1, pl.ds(row0, rows), :] = b_top.astype(BF16)
    dst_ref[ja + 1, pl.ds(KEYS_PAD + row0, rows), :] = b_bot.astype(BF16)


def _attend_pairs(q_ref, z_ref, cos, sin_signed, sinks_ref, k2_ref, v2_ref, o_ref, valid, group=8):
    rows = q_ref.shape[0]
    lane = lax.broadcasted_iota(jnp.int32, (rows, LANES), 1)
    lo = lane < ATT_HEAD_DIM
    neg_inf = jnp.full((rows, PAIR * KEYS_PAD), -jnp.inf, F32)

    def kv_of(p):
        return p // (N_PAIRS // ATT_KV_HEADS)

    def scores(p):
        cols = slice(p * LANES, (p + 1) * LANES)
        q = _rope(q_ref[:, cols], cos, sin_signed) * ATT_HEAD_DIM ** -0.5
        return _dot_nt(q.astype(BF16), k2_ref[kv_of(p)])

    def softmax(pairs, s_list):
        heads = [(p, hh) for p in pairs for hh in range(PAIR)]
        sh = [jnp.where(valid, s, neg_inf)[:, hh * KEYS_PAD:(hh + 1) * KEYS_PAD]
              for s in s_list for hh in range(PAIR)]
        sink = [sinks_ref[PAIR * p + hh] for p, hh in heads]
        m = [jnp.maximum(jnp.max(x, axis=-1, keepdims=True), sk) for x, sk in zip(sh, sink)]
        ph = [jnp.exp(x - mx) for x, mx in zip(sh, m)]
        probs = [jnp.concatenate(ph[PAIR * i:PAIR * (i + 1)], axis=-1).astype(BF16) for i in range(len(pairs))]
        inv = [1.0 / (jnp.sum(x, axis=-1, keepdims=True) + jnp.exp(sk - mx)) for x, sk, mx in zip(ph, sink, m)]
        inv_den = [jnp.where(lo, inv[PAIR * i], inv[PAIR * i + 1]) for i in range(len(pairs))]
        return probs, inv_den

    def finish(p, probs, inv_den):
        cols = slice(p * LANES, (p + 1) * LANES)
        o = _dot(probs, v2_ref[kv_of(p)]) * inv_den
        o_ref[:, cols] = (o * _silu(z_ref[:, cols])).astype(o_ref.dtype)

    groups = [range(g, g + group) for g in range(0, N_PAIRS, group)]
    s_next = [scores(p) for p in groups[0]]
    for gi, pairs in enumerate(groups):
        s_cur = s_next
        if gi + 1 < len(groups):
            s_next = [scores(p) for p in groups[gi + 1]]
        probs, inv_den = softmax(pairs, s_cur)
        for i, p in enumerate(pairs):
            finish(p, probs[i], inv_den[i])


def _swa_prompt_kernel(sinks_ref, q_ref, z_ref, k_ref, v_ref, cos_ref, sin_ref,
                       o_ref, kout_ref, k2_ref, v2_ref, *, n_steps):
    s = pl.program_id(1)
    rows = STEP_CHUNKS * CHUNK

    @pl.when(s == 0)
    def _():
        k2_ref[...] = jnp.zeros_like(k2_ref)
        v2_ref[...] = jnp.zeros_like(v2_ref)

    cos = cos_ref[...]
    sin_signed = sin_ref[...]
    half = s % (KEY_SLOTS // STEP_CHUNKS)
    row0 = pl.multiple_of(half * rows, rows)
    for ti in range(ATT_KV_WIDTH // LANES):
        k_rot = _rope(k_ref[:, ti * LANES:(ti + 1) * LANES], cos, sin_signed)
        _store_keys(k2_ref, ti, row0, rows, k_rot)
        _store_keys(v2_ref, ti, row0, rows, v_ref[:, ti * LANES:(ti + 1) * LANES])

        @pl.when(s == n_steps - 1)
        def _():
            kout_ref[0, :, ti * LANES:(ti + 1) * LANES] = k_rot

    shape = (rows, PAIR * KEYS_PAD)
    key_slot = (lax.broadcasted_iota(jnp.int32, shape, 1) % KEYS_PAD) // CHUNK
    rel = key_slot % STEP_CHUNKS - jnp.where(key_slot // STEP_CHUNKS == half, 0, STEP_CHUNKS)
    q_chunk = lax.broadcasted_iota(jnp.int32, shape, 0) // CHUNK
    valid = (rel <= q_chunk) & (rel >= q_chunk - WINDOW // CHUNK) & (s * STEP_CHUNKS + rel >= 0)
    _attend_pairs(q_ref, z_ref, cos, sin_signed, sinks_ref, k2_ref, v2_ref, o_ref, valid)


def _swa_prompt(main, sinks, cos_t, sin_t, *, n, t):
    rows = STEP_CHUNKS * CHUNK
    assert t % rows == 0 and rows == WINDOW
    ns = t // rows
    kvw = ATT_KV_WIDTH
    grid_spec = pltpu.PrefetchScalarGridSpec(
        num_scalar_prefetch=1,
        grid=(n, ns),
        in_specs=[
            pl.BlockSpec((rows, ATT_WIDTH), lambda b, c, s: (b * ns + c, COL_QAT // ATT_WIDTH)),
            pl.BlockSpec((rows, ATT_WIDTH), lambda b, c, s: (b * ns + c, COL_ZAT // ATT_WIDTH)),
            pl.BlockSpec((rows, kvw), lambda b, c, s: (b * ns + c, COL_KAT // kvw)),
            pl.BlockSpec((rows, kvw), lambda b, c, s: (b * ns + c, COL_VAT // kvw)),
            pl.BlockSpec((rows, LANES), lambda b, c, s: (c, 0)),
            pl.BlockSpec((rows, LANES), lambda b, c, s: (c, 0)),
        ],
        out_specs=[
            pl.BlockSpec((rows, ATT_WIDTH), lambda b, c, s: (b * ns + c, 0)),
            pl.BlockSpec((1, WINDOW, kvw), lambda b, c, s: (b, 0, 0)),
        ],
        scratch_shapes=[
            pltpu.VMEM((ATT_KV_HEADS, PAIR * KEYS_PAD, LANES), BF16),
            pltpu.VMEM((ATT_KV_HEADS, PAIR * KEYS_PAD, LANES), BF16),
        ],
    )
    return pl.pallas_call(
        functools.partial(_swa_prompt_kernel, n_steps=ns),
        grid_spec=grid_spec,
        out_shape=[
            jax.ShapeDtypeStruct((n * t, ATT_WIDTH), BF16),
            jax.ShapeDtypeStruct((n, WINDOW, kvw), F32),
        ],
        compiler_params=_cparams(("arbitrary", "arbitrary")),
        name="swa_prompt",
    )(sinks, main, main, main, main, cos_t, sin_t)


def _swa_sample_kernel(sinks_ref, q_ref, z_ref, k_ref, v_ref, ck_ref, cv_ref, cos_ref, sin_ref,
                       o_ref, kout_ref, k2_ref, v2_ref, *, t, past):
    k2_ref[...] = jnp.zeros_like(k2_ref)
    v2_ref[...] = jnp.zeros_like(v2_ref)
    cos = cos_ref[...]
    sin_signed = sin_ref[...]
    for ti in range(ATT_KV_WIDTH // LANES):
        cols = slice(ti * LANES, (ti + 1) * LANES)
        k_rot = _rope(k_ref[:, cols], cos, sin_signed)
        _store_keys(k2_ref, ti, 0, past, ck_ref[0, :, cols])
        _store_keys(k2_ref, ti, past, t, k_rot)
        _store_keys(v2_ref, ti, 0, past, cv_ref[0, :, cols])
        _store_keys(v2_ref, ti, past, t, v_ref[:, cols])
        kout_ref[0, 0:past - t, cols] = ck_ref[0, t:past, cols]
        kout_ref[0, past - t:past, cols] = k_rot
    key_pos = lax.broadcasted_iota(jnp.int32, (t, PAIR * KEYS_PAD), 1) % KEYS_PAD
    valid = key_pos < past + t
    _attend_pairs(q_ref, z_ref, cos, sin_signed, sinks_ref, k2_ref, v2_ref, o_ref, valid)


def _swa_sample(main, sinks, cache_k, cache_v, cos_t, sin_t, *, n, t):
    past = cache_k.shape[1]
    assert past + t <= KEYS_PAD and t <= past
    kvw = ATT_KV_WIDTH
    grid_spec = pltpu.PrefetchScalarGridSpec(
        num_scalar_prefetch=1,
        grid=(n,),
        in_specs=[
            pl.BlockSpec((t, ATT_WIDTH), lambda b, s: (b, COL_QAT // ATT_WIDTH)),
            pl.BlockSpec((t, ATT_WIDTH), lambda b, s: (b, COL_ZAT // ATT_WIDTH)),
            pl.BlockSpec((t, kvw), lambda b, s: (b, COL_KAT // kvw)),
            pl.BlockSpec((t, kvw), lambda b, s: (b, COL_VAT // kvw)),
            pl.BlockSpec((1, past, kvw), lambda b, s: (b, 0, 0)),
            pl.BlockSpec((1, past, kvw), lambda b, s: (b, 0, 0)),
            pl.BlockSpec((t, LANES), lambda b, s: (0, 0)),
            pl.BlockSpec((t, LANES), lambda b, s: (0, 0)),
        ],
        out_specs=[
            pl.BlockSpec((t, ATT_WIDTH), lambda b, s: (b, 0)),
            pl.BlockSpec((1, past, kvw), lambda b, s: (b, 0, 0)),
        ],
        scratch_shapes=[
            pltpu.VMEM((ATT_KV_HEADS, PAIR * KEYS_PAD, LANES), BF16),
            pltpu.VMEM((ATT_KV_HEADS, PAIR * KEYS_PAD, LANES), BF16),
        ],
    )
    return pl.pallas_call(
        functools.partial(_swa_sample_kernel, t=t, past=past),
        grid_spec=grid_spec,
        out_shape=[
            jax.ShapeDtypeStruct((n * t, ATT_WIDTH), BF16),
            jax.ShapeDtypeStruct((n, past, kvw), F32),
        ],
        compiler_params=_cparams(("arbitrary",)),
        name="swa_sample",
    )(sinks, main, main, main, main, cache_k, cache_v, cos_t, sin_t)


def _out_proj_kernel(odn_ref, oat_ref, w_ref, x_ref, mod_ref, gpost_ref, *rest, tt, nk, emit_h):
    if emit_h:
        modn_ref, gpre_ref, xo_ref, h_ref = rest
    else:
        (xo_ref,) = rest
    kk = pl.program_id(1)
    half = nk // 2

    @pl.when(kk == 0)
    def _():
        xo_ref[...] = jnp.zeros_like(xo_ref)

    @pl.when(kk < half)
    def _():
        xo_ref[...] += _dot(odn_ref[...], w_ref[...])

    @pl.when(kk >= half)
    def _():
        xo_ref[...] += _dot(oat_ref[...], w_ref[...])

    @pl.when(kk == nk - 1)
    def _():
        def body(chunks):
            ys = [xo_ref[rows, :] for rows, _ in chunks]
            x_new = [x_ref[rows, :] + mod_ref[b, 2] * (y * r * gpost_ref[...])
                     for (rows, b), y, r in zip(chunks, ys, _inv_rms(ys))]
            for (rows, _), xn in zip(chunks, x_new):
                xo_ref[rows, :] = xn
            if emit_h:
                for (rows, b), xn, r in zip(chunks, x_new, _inv_rms(x_new)):
                    h_ref[rows, :] = (xn * r * gpre_ref[...] * (1.0 + modn_ref[b, 1])
                                      + modn_ref[b, 0]).astype(BF16)

        _row_chunks(xo_ref.shape[0], tt, body)


def _out_proj(o_dn, o_at, w_out_b, layer, x2, mod, g_post, mod_next, g_pre_next, *, nb, tt, tk):
    m, d = x2.shape
    tm = nb * tt
    nk = (DN_WIDTH + ATT_WIDTH) // tk
    half = nk // 2
    tiles_per_batch = (m // mod.shape[0]) // tt
    emit_h = mod_next is not None
    mod_spec = pl.BlockSpec((nb, 3, ROW_CHUNK, d), lambda i, k: (i // tiles_per_batch, 0, 0, 0))
    vec_spec = pl.BlockSpec((ROW_CHUNK, d), lambda i, k: (0, 0))
    row_spec = pl.BlockSpec((tm, d), lambda i, k: (i, 0))
    in_specs = [
        pl.BlockSpec((tm, tk), lambda i, k: (i, jnp.minimum(k, half - 1))),
        pl.BlockSpec((tm, tk), lambda i, k: (i, jnp.maximum(k - half, 0))),
        pl.BlockSpec((None, tk, d), lambda i, k: (layer, k, 0)),
        row_spec, mod_spec, vec_spec,
    ]
    args = [o_dn, o_at, w_out_b, x2, mod, _rows_bcast(g_post)]
    out_specs = [row_spec]
    out_shape = [jax.ShapeDtypeStruct((m, d), F32)]
    if emit_h:
        in_specs += [mod_spec, vec_spec]
        args += [mod_next, _rows_bcast(g_pre_next)]
        out_specs.append(row_spec)
        out_shape.append(jax.ShapeDtypeStruct((m, d), BF16))
    outs = pl.pallas_call(
        functools.partial(_out_proj_kernel, tt=tt, nk=nk, emit_h=emit_h),
        grid=(m // tm, nk),
        in_specs=in_specs,
        out_specs=out_specs,
        out_shape=out_shape,
        compiler_params=_cparams(("arbitrary", "arbitrary")),
        name="out_proj",
    )(*args)
    return (outs[0], outs[1]) if emit_h else (outs[0], None)


def _rope_tables(pos):
    half = ATT_HEAD_DIM // 2
    inv = ROPE_THETA ** (-jnp.arange(half, dtype=F32) / half)
    ang = pos.astype(F32)[:, None] * inv[None, :]
    cos = jnp.cos(ang)
    sin = jnp.sin(ang)
    reps = LANES // ATT_HEAD_DIM
    return jnp.tile(jnp.concatenate([cos, cos], axis=-1), (1, reps)), \
        jnp.tile(jnp.concatenate([-sin, sin], axis=-1), (1, reps))


def _layer(x2, h, mod, mod_next, conv_buf, s0, cache_k, cache_v, pos, lw, layer, *, n, t, prompt):
    (g_post, g_pre_next, w_dn, w_at, w_ba, conv_w, gpar, dn_norm, sinks, w_out_b) = lw
    if prompt:
        nb, tt, tm_in, cs = 1, 512, min(1024, n * t), CHUNK
    else:
        nb, tt, tm_in, cs = n, t, n * t, min(CHUNK, t)
    (main_dn,) = _in_proj(h, w_dn, layer, tm=tm_in, tn=1024)
    main, ba = _in_proj(h, w_at, layer, tm=tm_in, tn=768, w_ba=w_ba)

    conv_buf8 = jnp.pad(conv_buf, ((0, 0), (SUBLANES - (CONV_W - 1), 0), (0, 0)))
    main_dn3 = main_dn.reshape(n, t, DN_PART_DIM)
    o_dn, s_new = _deltanet(main_dn3, ba.reshape(n, t, BA_DIM), conv_buf8, conv_w, gpar, dn_norm, s0,
                            cs=cs, nbk=DN_SEQS_PER_STEP)
    o_dn = o_dn.reshape(n * t, DN_WIDTH)
    conv_new = main_dn3[:, t - (CONV_W - 1):, COL_QKV:COL_QKV + DN_CONV_DIM]

    cos_t, sin_t = _rope_tables(pos)
    v_rows = main.reshape(n, t, AT_PART_DIM)[:, :, COL_VAT:COL_VAT + ATT_KV_WIDTH]
    if prompt:
        o_at, k_new = _swa_prompt(main, sinks, cos_t, sin_t, n=n, t=t)
        v_new = v_rows[:, t - WINDOW:]
    else:
        past = cache_k.shape[1]
        ck = cache_k.reshape(n, past, ATT_KV_WIDTH)
        cv = cache_v.reshape(n, past, ATT_KV_WIDTH)
        o_at, k_new = _swa_sample(main, sinks, ck, cv, cos_t, sin_t, n=n, t=t)
        v_new = jnp.concatenate([cv, v_rows], axis=1)[:, -past:]
    rows = k_new.shape[1]
    k_new = k_new.reshape(n, rows, ATT_KV_HEADS, ATT_HEAD_DIM)
    v_new = v_new.reshape(n, rows, ATT_KV_HEADS, ATT_HEAD_DIM)

    x_new, h_next = _out_proj(o_dn, o_at, w_out_b, layer, x2, mod, g_post, mod_next, g_pre_next,
                              nb=nb, tt=tt, tk=512)
    return x_new, h_next, conv_new, s_new, k_new, v_new


def kernel(x_prompt, x_sample, state_conv, state_dn, cache_k, cache_v, c_prompt, c_sample,
           w_ada, b_ada, g_pre, g_post, w_in, conv_w, a_log, dt_bias, dn_norm, sinks, w_out):
    depth = w_in.shape[0]
    bp, tp, d = x_prompt.shape
    bs, ts, _ = x_sample.shape
    pos_p = jnp.arange(tp)
    pos_s = PAST_LEN + jnp.arange(ts)

    w_dn, w_at, w_ba = _w_prep(w_in)
    w_out_b = w_out.astype(BF16)
    pad_rows = (-(bp + bs)) % (2 * SUBLANES)
    c_all = jnp.concatenate([c_prompt, c_sample, jnp.zeros((pad_rows, d), F32)], axis=0)
    gate_pad = jnp.zeros((depth, BA_DIM - 2 * DN_HEADS), F32)
    head_pad = jnp.zeros((depth, DN_HEADS), F32)
    gpar = jnp.stack([jnp.concatenate([head_pad, a_log, gate_pad], axis=-1),
                      jnp.concatenate([head_pad, dt_bias, gate_pad], axis=-1)], axis=1)

    zero_conv = jnp.zeros((bp, CONV_W - 1, DN_CONV_DIM), F32)
    zero_state = jnp.zeros((bp, DN_HEADS, DN_HEAD_DIM, DN_HEAD_DIM), F32)

    ada = _ada(c_all, w_ada, b_ada)
    mod_p = _rows_bcast(ada[:, :bp].reshape(depth, bp, 3, d))
    mod_s = _rows_bcast(ada[:, bp:bp + bs].reshape(depth, bs, 3, d))

    xp = x_prompt.reshape(bp * tp, d)
    xs = x_sample.reshape(bs * ts, d)
    hp = _prenorm(xp, mod_p[0], g_pre[0], nb=1, tt=512)
    hs = _prenorm(xs, mod_s[0], g_pre[0], nb=bs, tt=ts)
    outs_p, outs_s = [], []
    for l in range(depth):
        last = l == depth - 1
        lw = (g_post[l], None if last else g_pre[l + 1], w_dn, w_at, w_ba, conv_w[l], gpar[l], dn_norm[l],
              sinks[l], w_out_b)
        xp, hp, *rest_p = _layer(xp, hp, mod_p[l], None if last else mod_p[l + 1], zero_conv, zero_state,
                                 None, None, pos_p, lw, l, n=bp, t=tp, prompt=True)
        xs, hs, *rest_s = _layer(xs, hs, mod_s[l], None if last else mod_s[l + 1], state_conv[l], state_dn[l],
                                 cache_k[l], cache_v[l], pos_s, lw, l, n=bs, t=ts, prompt=False)
        outs_p.append(rest_p)
        outs_s.append(rest_s)
    xp = xp.reshape(bp, tp, d)
    xs = xs.reshape(bs, ts, d)

    def stack(outs, i):
        return jnp.stack([o[i] for o in outs])

    return (xp, xs,
            stack(outs_p, 0), stack(outs_p, 1), stack(outs_p, 2), stack(outs_p, 3),
            stack(outs_s, 0), stack(outs_s, 1), stack(outs_s, 2), stack(outs_s, 3))
```

```python
import functools
import math

import jax
import jax.numpy as jnp
from jax import lax
from jax.experimental import pallas as pl
from jax.experimental.pallas import tpu as pltpu

F32 = jnp.float32
BF16 = jnp.bfloat16

D_MODEL = 4096
CHUNK = 64
DN_WIDTH = D_MODEL // 2
DN_HEAD_DIM = 128
DN_HEADS = DN_WIDTH // DN_HEAD_DIM
CONV_W = 4
DN_CONV_DIM = 3 * DN_WIDTH
ATT_WIDTH = D_MODEL - DN_WIDTH
ATT_HEAD_DIM = 64
ATT_Q_HEADS = ATT_WIDTH // ATT_HEAD_DIM
ATT_KV_HEADS = ATT_Q_HEADS // 8
ATT_KV_WIDTH = ATT_KV_HEADS * ATT_HEAD_DIM
WINDOW = 128
ROPE_THETA = 10000.0
NORM_EPS = 1e-6
PAST_LEN = 2048

LANES = 128
SUBLANES = 8
VMEM_LIMIT_BYTES = 56 * 1024 * 1024

COL_QKV = 0
COL_ZDN = COL_QKV + DN_CONV_DIM
DN_PART_DIM = COL_ZDN + DN_WIDTH
COL_QAT = 0
COL_ZAT = COL_QAT + ATT_WIDTH
COL_KAT = COL_ZAT + ATT_WIDTH
COL_VAT = COL_KAT + ATT_KV_WIDTH
AT_PART_DIM = COL_VAT + ATT_KV_WIDTH
BA_DIM = LANES
SRC_BETA = DN_PART_DIM
SRC_DECAY = SRC_BETA + DN_HEADS
SRC_QAT = SRC_DECAY + DN_HEADS
SRC_KAT = SRC_QAT + ATT_WIDTH
SRC_VAT = SRC_KAT + ATT_KV_WIDTH
SRC_ZAT = SRC_VAT + ATT_KV_WIDTH
IN_DIM = SRC_ZAT + ATT_WIDTH

ROW_CHUNK = 8
ROW_GROUP = 8
PAIR = LANES // ATT_HEAD_DIM
N_PAIRS = ATT_Q_HEADS // PAIR
KEY_SLOTS = 4
KEYS_PAD = KEY_SLOTS * CHUNK
STEP_CHUNKS = 2
DN_SEQS_PER_STEP = 2


def _cparams(sem):
    return pltpu.CompilerParams(dimension_semantics=sem, vmem_limit_bytes=VMEM_LIMIT_BYTES)


def _dot(a, b):
    return jnp.dot(a, b, preferred_element_type=F32)


def _dot_nt(a, b):
    return lax.dot_general(a, b, (((1,), (1,)), ((), ())), preferred_element_type=F32)


def _dot_tn(a, b):
    return lax.dot_general(a, b, (((0,), (0,)), ((), ())), preferred_element_type=F32)


def _split(a):
    hi = a.astype(BF16)
    lo = (a - hi.astype(F32)).astype(BF16)
    return hi, lo


def _dot_split(a, b):
    ah, al = _split(a)
    bh, bl = _split(b)
    return _dot(ah, bh) + (_dot(ah, bl) + _dot(al, bh))


def _sigmoid(x):
    return 1.0 / (1.0 + jnp.exp(-x))


def _silu(x):
    return x * _sigmoid(x)


def _ada_kernel(c_ref, w_ref, b_ref, o_ref):
    s = _silu(c_ref[...]).astype(BF16)
    o_ref[...] = _dot(s, w_ref[...].astype(BF16)) + b_ref[...]


def _ada(c_all, w_ada, b_ada):
    rows, d = c_all.shape
    depth, _, n_out = w_ada.shape
    tn = 512
    return pl.pallas_call(
        _ada_kernel,
        grid=(depth, n_out // tn),
        in_specs=[
            pl.BlockSpec((rows, d), lambda l, j: (0, 0)),
            pl.BlockSpec((None, d, tn), lambda l, j: (l, 0, j)),
            pl.BlockSpec((None, 1, tn), lambda l, j: (l, 0, j)),
        ],
        out_specs=pl.BlockSpec((None, rows, tn), lambda l, j: (l, 0, j)),
        out_shape=jax.ShapeDtypeStruct((depth, rows, n_out), F32),
        compiler_params=_cparams(("arbitrary", "arbitrary")),
        name="ada",
    )(c_all, w_ada, b_ada.reshape(depth, 1, n_out))


def _rows_bcast(v):
    return jnp.broadcast_to(v[..., None, :], v.shape[:-1] + (ROW_CHUNK, v.shape[-1]))


def _row_chunks(n_rows, rows_per_batch, body):
    chunks_per_batch = rows_per_batch // ROW_CHUNK
    group = math.gcd(ROW_GROUP, n_rows // ROW_CHUNK)

    def step(r, carry):
        chunks = []
        for u in range(group):
            ci = r * group + u
            chunks.append((pl.ds(pl.multiple_of(ci * ROW_CHUNK, ROW_CHUNK), ROW_CHUNK), ci // chunks_per_batch))
        body(chunks)
        return carry

    lax.fori_loop(0, n_rows // (ROW_CHUNK * group), step, 0)


def _inv_rms(xs):
    return [lax.rsqrt(jnp.mean(x * x, axis=-1, keepdims=True) + NORM_EPS) for x in xs]


def _prenorm_kernel(x_ref, mod_ref, g_ref, h_ref, *, tt):
    def body(chunks):
        xs = [x_ref[rows, :] for rows, _ in chunks]
        for (rows, b), x, r in zip(chunks, xs, _inv_rms(xs)):
            h_ref[rows, :] = (x * r * g_ref[...] * (1.0 + mod_ref[b, 1]) + mod_ref[b, 0]).astype(BF16)

    _row_chunks(x_ref.shape[0], tt, body)


def _prenorm(x2, mod, g_pre, *, nb, tt):
    m, d = x2.shape
    tm = nb * tt
    tiles_per_batch = (m // mod.shape[0]) // tt
    return pl.pallas_call(
        functools.partial(_prenorm_kernel, tt=tt),
        grid=(m // tm,),
        in_specs=[
            pl.BlockSpec((tm, d), lambda i: (i, 0)),
            pl.BlockSpec((nb, 3, ROW_CHUNK, d), lambda i: (i // tiles_per_batch, 0, 0, 0)),
            pl.BlockSpec((ROW_CHUNK, d), lambda i: (0, 0)),
        ],
        out_specs=pl.BlockSpec((tm, d), lambda i: (i, 0)),
        out_shape=jax.ShapeDtypeStruct((m, d), BF16),
        compiler_params=_cparams(("arbitrary",)),
        name="prenorm",
    )(x2, mod, _rows_bcast(g_pre))


def _cast_kernel(starts_ref, w_ref, o_ref, *, valid_rows):
    w = w_ref[0]
    if valid_rows < w.shape[0]:
        row = lax.broadcasted_iota(jnp.int32, w.shape, 0)
        w = jnp.where(row < valid_rows, w, 0.0)
    o_ref[...] = w.astype(BF16)


def _cast_rows(w_t, starts, tr, valid_rows=None):
    depth, _, d = w_t.shape
    n_blk = len(starts)
    grid_spec = pltpu.PrefetchScalarGridSpec(
        num_scalar_prefetch=1,
        grid=(depth, n_blk),
        in_specs=[pl.BlockSpec((pl.Element(1), pl.Element(tr), pl.Element(d)),
                               lambda l, i, st: (l, pl.multiple_of(st[i], 2 * SUBLANES), 0))],
        out_specs=pl.BlockSpec((None, tr, d), lambda l, i, st: (l, i, 0)),
    )
    return pl.pallas_call(
        functools.partial(_cast_kernel, valid_rows=tr if valid_rows is None else valid_rows),
        grid_spec=grid_spec,
        out_shape=jax.ShapeDtypeStruct((depth, n_blk * tr, d), BF16),
        compiler_params=_cparams(("arbitrary", "arbitrary")),
        name="w_prep",
    )(jnp.asarray(starts, jnp.int32), w_t)


def _w_prep(w_in):
    assert w_in.shape[-1] == IN_DIM
    w_t = jnp.swapaxes(w_in, 1, 2)
    tr = ATT_KV_WIDTH
    w_dn = _cast_rows(w_t, list(range(0, DN_PART_DIM, tr)), tr)
    at_rows = ([SRC_QAT + r for r in range(0, ATT_WIDTH, tr)] + [SRC_ZAT + r for r in range(0, ATT_WIDTH, tr)]
               + [SRC_KAT, SRC_VAT])
    w_at = _cast_rows(w_t, at_rows, tr)
    w_ba = _cast_rows(w_t, [SRC_BETA], BA_DIM, valid_rows=2 * DN_HEADS)
    return w_dn, w_at, w_ba


def _in_proj_kernel(h_ref, w_ref, *rest):
    if len(rest) == 3:
        wba_ref, o_ref, oba_ref = rest

        @pl.when(pl.program_id(1) == 0)
        def _():
            oba_ref[...] = _dot_nt(h_ref[...], wba_ref[...])
    else:
        (o_ref,) = rest
    o_ref[...] = _dot_nt(h_ref[...], w_ref[...])


def _in_proj(h, w, layer, *, tm, tn, w_ba=None):
    m, d = h.shape
    n_out = w.shape[1]
    in_specs = [
        pl.BlockSpec((tm, d), lambda i, j: (i, 0)),
        pl.BlockSpec((None, tn, d), lambda i, j: (layer, j, 0)),
    ]
    out_specs = [pl.BlockSpec((tm, tn), lambda i, j: (i, j))]
    out_shape = [jax.ShapeDtypeStruct((m, n_out), F32)]
    args = [h, w]
    if w_ba is not None:
        in_specs.append(pl.BlockSpec((None, BA_DIM, d), lambda i, j: (layer, 0, 0)))
        out_specs.append(pl.BlockSpec((tm, BA_DIM), lambda i, j: (i, 0)))
        out_shape.append(jax.ShapeDtypeStruct((m, BA_DIM), F32))
        args.append(w_ba)
    return pl.pallas_call(
        _in_proj_kernel,
        grid=(m // tm, n_out // tn),
        in_specs=in_specs,
        out_specs=out_specs,
        out_shape=out_shape,
        compiler_params=_cparams(("arbitrary", "arbitrary")),
        name="in_proj",
    )(*args)


def _deltanet_kernel(qkv_ref, qkvn_ref, z_ref, ba_ref, cbuf_ref, cw_ref, gpar_ref, ng_ref, s0_ref,
                     o_ref, s_ref, ext_ref, gct_ref, q_scr, k_scr, v_scr, *, cs, nbk):
    c = pl.program_id(1)
    hd = DN_HEAD_DIM
    units = [(b, h) for b in range(nbk) for h in range(DN_HEADS)]
    idx = range(len(units))
    hist = SUBLANES - (CONV_W - 1)

    def conv_silu(b, col0):
        cols = slice(col0, col0 + hd)
        acc = ext_ref[b, hist:hist + cs, cols] * cw_ref[0:1, cols]
        for j in range(1, CONV_W):
            acc = acc + ext_ref[b, hist + j:hist + j + cs, cols] * cw_ref[j:j + 1, cols]
        return _silu(acc)

    def front_end(slot):
        for u, (b, h) in enumerate(units):
            qh = conv_silu(b, h * hd)
            kh = conv_silu(b, DN_WIDTH + h * hd)
            q_scr[slot, u] = qh * lax.rsqrt(jnp.sum(qh * qh, axis=-1, keepdims=True) + NORM_EPS) * hd ** -0.5
            k_scr[slot, u] = kh * lax.rsqrt(jnp.sum(kh * kh, axis=-1, keepdims=True) + NORM_EPS)
            v_scr[slot, u] = conv_silu(b, 2 * DN_WIDTH + h * hd)

    @pl.when(c == 0)
    def _():
        s_ref[...] = s0_ref[...]
        ext_ref[:, 0:SUBLANES, :] = cbuf_ref[...]
        ext_ref[:, SUBLANES:SUBLANES + cs, :] = qkv_ref[...]
        front_end(0)

    cur = c % 2

    row = lax.broadcasted_iota(jnp.int32, (cs, LANES), 0)
    lane = lax.broadcasted_iota(jnp.int32, (cs, LANES), 1)
    beta_all, gc_all = [], []
    for b in range(nbk):
        ba = ba_ref[b]
        xg = ba + gpar_ref[1:2, :]
        softplus = jnp.maximum(xg, 0.0) + jnp.log(1.0 + jnp.exp(-jnp.abs(xg)))
        gc = -jnp.exp(gpar_ref[0:1, :]) * softplus
        shift = 1
        while shift < cs:
            gc = gc + jnp.where(row >= shift, pltpu.roll(gc, shift, axis=0), 0.0)
            shift *= 2
        gct_ref[b] = gc.T
        beta_all.append(_sigmoid(ba))
        gc_all.append(gc)

    ri = lax.broadcasted_iota(jnp.int32, (cs, cs), 0)
    ci = lax.broadcasted_iota(jnp.int32, (cs, cs), 1)
    causal = ri >= ci
    strict = ri > ci
    eye = jnp.where(ri == ci, 1.0, 0.0).astype(F32)
    n_doublings = int(math.log2(cs)) - 1

    q = [q_scr[cur, u] for u in idx]
    k = [k_scr[cur, u] for u in idx]
    v = [v_scr[cur, u] for u in idx]

    beta = [jnp.sum(jnp.where(lane == h, beta_all[b], 0.0), axis=-1, keepdims=True) for b, h in units]
    gcol = [jnp.sum(jnp.where(lane == DN_HEADS + h, gc_all[b], 0.0), axis=-1, keepdims=True) for b, h in units]
    grow = [gct_ref[b, DN_HEADS + h:DN_HEADS + h + 1, :] for b, h in units]
    glast = [g[:, cs - 1:cs] for g in grow]
    gamma = [jnp.where(causal, jnp.exp(gcol[u] - grow[u]), 0.0) for u in idx]
    ecol = [jnp.exp(g) for g in gcol]

    kb = [x.astype(BF16) for x in k]
    kq = [_dot_nt(jnp.concatenate([kb[u], q[u].astype(BF16)], axis=0), kb[u]) for u in idx]

    ext_ref[:, 0:SUBLANES, :] = qkv_ref[:, cs - SUBLANES:cs, :]
    ext_ref[:, SUBLANES:SUBLANES + cs, :] = qkvn_ref[...]
    front_end(1 - cur)

    a_mat = [jnp.where(strict, beta[u] * kq[u][:cs] * gamma[u], 0.0) for u in idx]
    qk = [kq[u][cs:] * gamma[u] for u in idx]

    inv = [eye - a for a in a_mat]
    apow = [a.astype(BF16) for a in a_mat]
    for _ in range(n_doublings):
        apow = [_dot(a, a).astype(BF16) for a in apow]
        inv = [inv[u] + _dot(inv[u].astype(BF16), apow[u]) for u in idx]
    resid = [eye - inv[u] - _dot_split(a_mat[u], inv[u]) for u in idx]
    inv = [inv[u] + _dot(inv[u].astype(BF16), resid[u].astype(BF16)) for u in idx]

    rhs = [jnp.concatenate([beta[u] * v[u], beta[u] * ecol[u] * k[u]], axis=-1).astype(BF16) for u in idx]
    uw = [_dot(inv[u].astype(BF16), rhs[u]) for u in idx]

    state = [s_ref[b, h] for b, h in units]
    ws = [_dot(jnp.concatenate([uw[u][:, hd:], q[u] * ecol[u]], axis=0).astype(BF16), state[u].astype(BF16))
          for u in idx]
    v_new = [(uw[u][:, :hd] - ws[u][:cs]).astype(BF16) for u in idx]
    o = [ws[u][cs:] + _dot(qk[u].astype(BF16), v_new[u]) for u in idx]
    for u, (b, h) in enumerate(units):
        kdec = k[u] * jnp.exp(glast[u] - gcol[u])
        s_ref[b, h] = state[u] * jnp.exp(glast[u]) + _dot_tn(kdec.astype(BF16), v_new[u])
    for u, (b, h) in enumerate(units):
        oh = o[u] * lax.rsqrt(jnp.mean(o[u] * o[u], axis=-1, keepdims=True) + NORM_EPS) * ng_ref[...]
        oh = oh * _silu(z_ref[b, :, h * hd:(h + 1) * hd])
        o_ref[b, :, h * hd:(h + 1) * hd] = oh.astype(o_ref.dtype)


def _deltanet(main3, ba3, conv_buf8, conv_w, gpar, dn_norm, s0, *, cs, nbk):
    n, t, _ = main3.shape
    nc = t // cs
    hd = DN_HEAD_DIM
    return pl.pallas_call(
        functools.partial(_deltanet_kernel, cs=cs, nbk=nbk),
        grid=(n // nbk, nc),
        in_specs=[
            pl.BlockSpec((nbk, cs, DN_CONV_DIM), lambda b, c: (b, c, COL_QKV // DN_CONV_DIM)),
            pl.BlockSpec((nbk, cs, DN_CONV_DIM), lambda b, c: (b, jnp.minimum(c + 1, nc - 1), COL_QKV // DN_CONV_DIM)),
            pl.BlockSpec((nbk, cs, DN_WIDTH), lambda b, c: (b, c, COL_ZDN // DN_WIDTH)),
            pl.BlockSpec((nbk, cs, BA_DIM), lambda b, c: (b, c, 0)),
            pl.BlockSpec((nbk, SUBLANES, DN_CONV_DIM), lambda b, c: (b, 0, 0)),
            pl.BlockSpec((CONV_W, DN_CONV_DIM), lambda b, c: (0, 0)),
            pl.BlockSpec((2, BA_DIM), lambda b, c: (0, 0)),
            pl.BlockSpec((1, hd), lambda b, c: (0, 0)),
            pl.BlockSpec((nbk, DN_HEADS, hd, hd), lambda b, c: (b, 0, 0, 0)),
        ],
        out_specs=[
            pl.BlockSpec((nbk, cs, DN_WIDTH), lambda b, c: (b, c, 0)),
            pl.BlockSpec((nbk, DN_HEADS, hd, hd), lambda b, c: (b, 0, 0, 0)),
        ],
        out_shape=[
            jax.ShapeDtypeStruct((n, t, DN_WIDTH), BF16),
            jax.ShapeDtypeStruct((n, DN_HEADS, hd, hd), F32),
        ],
        scratch_shapes=[
            pltpu.VMEM((nbk, SUBLANES + cs, DN_CONV_DIM), F32),
            pltpu.VMEM((nbk, LANES, cs), F32),
        ] + [pltpu.VMEM((2, nbk * DN_HEADS, cs, hd), F32)] * 3,
        compiler_params=_cparams(("arbitrary", "arbitrary")),
        name="deltanet",
    )(main3, main3, main3, ba3, conv_buf8, conv_w, gpar, dn_norm.reshape(1, hd), s0)


def _rope(x, cos, sin_signed):
    lane = lax.broadcasted_iota(jnp.int32, x.shape, 1)
    half = ATT_HEAD_DIM // 2
    swapped = jnp.where((lane % ATT_HEAD_DIM) < half,
                        pltpu.roll(x, LANES - half, axis=1), pltpu.roll(x, half, axis=1))
    return x * cos + swapped * sin_signed


def _split_heads(tile):
    lane = lax.broadcasted_iota(jnp.int32, tile.shape, 1)
    lo = lane < ATT_HEAD_DIM
    rolled = pltpu.roll(tile, ATT_HEAD_DIM, axis=1)
    zero = jnp.zeros_like(tile)
    return (jnp.where(lo, tile, zero), jnp.where(lo, zero, rolled),
            jnp.where(lo, rolled, zero), jnp.where(lo, zero, tile))


def _store_keys(dst_ref, tile_idx, row0, rows, tile):
    a_top, a_bot, b_top, b_bot = _split_heads(tile)
    ja = PAIR * tile_idx
    dst_ref[ja, pl.ds(row0, rows), :] = a_top.astype(BF16)
    dst_ref[ja, pl.ds(KEYS_PAD + row0, rows), :] = a_bot.astype(BF16)
    dst_ref[ja + 1, pl.ds(row0, rows), :] = b_top.astype(BF16)
    dst_ref[ja + 1, pl.ds(KEYS_PAD + row0, rows), :] = b_bot.astype(BF16)


def _attend_pairs(q_ref, z_ref, cos, sin_signed, sinks_ref, k2_ref, v2_ref, o_ref, valid, group=4):
    rows = q_ref.shape[0]
    lane = lax.broadcasted_iota(jnp.int32, (rows, LANES), 1)
    lo = lane < ATT_HEAD_DIM
    neg_inf = jnp.full((rows, PAIR * KEYS_PAD), -jnp.inf, F32)

    def kv_of(p):
        return p // (N_PAIRS // ATT_KV_HEADS)

    def scores(p):
        cols = slice(p * LANES, (p + 1) * LANES)
        q = _rope(q_ref[:, cols], cos, sin_signed) * ATT_HEAD_DIM ** -0.5
        return _dot_nt(q.astype(BF16), k2_ref[kv_of(p)])

    def softmax(pairs, s_list):
        heads = [(p, hh) for p in pairs for hh in range(PAIR)]
        sh = [jnp.where(valid, s, neg_inf)[:, hh * KEYS_PAD:(hh + 1) * KEYS_PAD]
              for s in s_list for hh in range(PAIR)]
        sink = [sinks_ref[PAIR * p + hh] for p, hh in heads]
        m = [jnp.maximum(jnp.max(x, axis=-1, keepdims=True), sk) for x, sk in zip(sh, sink)]
        ph = [jnp.exp(x - mx) for x, mx in zip(sh, m)]
        probs = [jnp.concatenate(ph[PAIR * i:PAIR * (i + 1)], axis=-1).astype(BF16) for i in range(len(pairs))]
        inv = [1.0 / (jnp.sum(x, axis=-1, keepdims=True) + jnp.exp(sk - mx)) for x, sk, mx in zip(ph, sink, m)]
        inv_den = [jnp.where(lo, inv[PAIR * i], inv[PAIR * i + 1]) for i in range(len(pairs))]
        return probs, inv_den

    def finish(p, probs, inv_den):
        cols = slice(p * LANES, (p + 1) * LANES)
        o = _dot(probs, v2_ref[kv_of(p)]) * inv_den
        o_ref[:, cols] = (o * _silu(z_ref[:, cols])).astype(o_ref.dtype)

    groups = [range(g, g + group) for g in range(0, N_PAIRS, group)]
    s_next = [scores(p) for p in groups[0]]
    for gi, pairs in enumerate(groups):
        s_cur = s_next
        if gi + 1 < len(groups):
            s_next = [scores(p) for p in groups[gi + 1]]
        probs, inv_den = softmax(pairs, s_cur)
        for i, p in enumerate(pairs):
            finish(p, probs[i], inv_den[i])


def _swa_prompt_kernel(sinks_ref, q_ref, z_ref, k_ref, v_ref, cos_ref, sin_ref,
                       o_ref, kout_ref, k2_ref, v2_ref, *, n_steps):
    s = pl.program_id(1)
    rows = STEP_CHUNKS * CHUNK

    @pl.when(s == 0)
    def _():
        k2_ref[...] = jnp.zeros_like(k2_ref)
        v2_ref[...] = jnp.zeros_like(v2_ref)

    cos = cos_ref[...]
    sin_signed = sin_ref[...]
    half = s % (KEY_SLOTS // STEP_CHUNKS)
    row0 = pl.multiple_of(half * rows, rows)
    for ti in range(ATT_KV_WIDTH // LANES):
        k_rot = _rope(k_ref[:, ti * LANES:(ti + 1) * LANES], cos, sin_signed)
        _store_keys(k2_ref, ti, row0, rows, k_rot)
        _store_keys(v2_ref, ti, row0, rows, v_ref[:, ti * LANES:(ti + 1) * LANES])

        @pl.when(s == n_steps - 1)
        def _():
            kout_ref[0, :, ti * LANES:(ti + 1) * LANES] = k_rot

    shape = (rows, PAIR * KEYS_PAD)
    key_slot = (lax.broadcasted_iota(jnp.int32, shape, 1) % KEYS_PAD) // CHUNK
    rel = key_slot % STEP_CHUNKS - jnp.where(key_slot // STEP_CHUNKS == half, 0, STEP_CHUNKS)
    q_chunk = lax.broadcasted_iota(jnp.int32, shape, 0) // CHUNK
    valid = (rel <= q_chunk) & (rel >= q_chunk - WINDOW // CHUNK) & (s * STEP_CHUNKS + rel >= 0)
    _attend_pairs(q_ref, z_ref, cos, sin_signed, sinks_ref, k2_ref, v2_ref, o_ref, valid)


def _swa_prompt(main, sinks, cos_t, sin_t, *, n, t):
    rows = STEP_CHUNKS * CHUNK
    assert t % rows == 0 and rows == WINDOW
    ns = t // rows
    kvw = ATT_KV_WIDTH
    grid_spec = pltpu.PrefetchScalarGridSpec(
        num_scalar_prefetch=1,
        grid=(n, ns),
        in_specs=[
            pl.BlockSpec((rows, ATT_WIDTH), lambda b, c, s: (b * ns + c, COL_QAT // ATT_WIDTH)),
            pl.BlockSpec((rows, ATT_WIDTH), lambda b, c, s: (b * ns + c, COL_ZAT // ATT_WIDTH)),
            pl.BlockSpec((rows, kvw), lambda b, c, s: (b * ns + c, COL_KAT // kvw)),
            pl.BlockSpec((rows, kvw), lambda b, c, s: (b * ns + c, COL_VAT // kvw)),
            pl.BlockSpec((rows, LANES), lambda b, c, s: (c, 0)),
            pl.BlockSpec((rows, LANES), lambda b, c, s: (c, 0)),
        ],
        out_specs=[
            pl.BlockSpec((rows, ATT_WIDTH), lambda b, c, s: (b * ns + c, 0)),
            pl.BlockSpec((1, WINDOW, kvw), lambda b, c, s: (b, 0, 0)),
        ],
        scratch_shapes=[
            pltpu.VMEM((ATT_KV_HEADS, PAIR * KEYS_PAD, LANES), BF16),
            pltpu.VMEM((ATT_KV_HEADS, PAIR * KEYS_PAD, LANES), BF16),
        ],
    )
    return pl.pallas_call(
        functools.partial(_swa_prompt_kernel, n_steps=ns),
        grid_spec=grid_spec,
        out_shape=[
            jax.ShapeDtypeStruct((n * t, ATT_WIDTH), BF16),
            jax.ShapeDtypeStruct((n, WINDOW, kvw), F32),
        ],
        compiler_params=_cparams(("arbitrary", "arbitrary")),
        name="swa_prompt",
    )(sinks, main, main, main, main, cos_t, sin_t)


def _swa_sample_kernel(sinks_ref, q_ref, z_ref, k_ref, v_ref, ck_ref, cv_ref, cos_ref, sin_ref,
                       o_ref, kout_ref, k2_ref, v2_ref, *, t, past):
    k2_ref[...] = jnp.zeros_like(k2_ref)
    v2_ref[...] = jnp.zeros_like(v2_ref)
    cos = cos_ref[...]
    sin_signed = sin_ref[...]
    for ti in range(ATT_KV_WIDTH // LANES):
        cols = slice(ti * LANES, (ti + 1) * LANES)
        k_rot = _rope(k_ref[:, cols], cos, sin_signed)
        _store_keys(k2_ref, ti, 0, past, ck_ref[0, :, cols])
        _store_keys(k2_ref, ti, past, t, k_rot)
        _store_keys(v2_ref, ti, 0, past, cv_ref[0, :, cols])
        _store_keys(v2_ref, ti, past, t, v_ref[:, cols])
        kout_ref[0, 0:past - t, cols] = ck_ref[0, t:past, cols]
        kout_ref[0, past - t:past, cols] = k_rot
    key_pos = lax.broadcasted_iota(jnp.int32, (t, PAIR * KEYS_PAD), 1) % KEYS_PAD
    valid = key_pos < past + t
    _attend_pairs(q_ref, z_ref, cos, sin_signed, sinks_ref, k2_ref, v2_ref, o_ref, valid)


def _swa_sample(main, sinks, cache_k, cache_v, cos_t, sin_t, *, n, t):
    past = cache_k.shape[1]
    assert past + t <= KEYS_PAD and t <= past
    kvw = ATT_KV_WIDTH
    grid_spec = pltpu.PrefetchScalarGridSpec(
        num_scalar_prefetch=1,
        grid=(n,),
        in_specs=[
            pl.BlockSpec((t, ATT_WIDTH), lambda b, s: (b, COL_QAT // ATT_WIDTH)),
            pl.BlockSpec((t, ATT_WIDTH), lambda b, s: (b, COL_ZAT // ATT_WIDTH)),
            pl.BlockSpec((t, kvw), lambda b, s: (b, COL_KAT // kvw)),
            pl.BlockSpec((t, kvw), lambda b, s: (b, COL_VAT // kvw)),
            pl.BlockSpec((1, past, kvw), lambda b, s: (b, 0, 0)),
            pl.BlockSpec((1, past, kvw), lambda b, s: (b, 0, 0)),
            pl.BlockSpec((t, LANES), lambda b, s: (0, 0)),
            pl.BlockSpec((t, LANES), lambda b, s: (0, 0)),
        ],
        out_specs=[
            pl.BlockSpec((t, ATT_WIDTH), lambda b, s: (b, 0)),
            pl.BlockSpec((1, past, kvw), lambda b, s: (b, 0, 0)),
        ],
        scratch_shapes=[
            pltpu.VMEM((ATT_KV_HEADS, PAIR * KEYS_PAD, LANES), BF16),
            pltpu.VMEM((ATT_KV_HEADS, PAIR * KEYS_PAD, LANES), BF16),
        ],
    )
    return pl.pallas_call(
        functools.partial(_swa_sample_kernel, t=t, past=past),
        grid_spec=grid_spec,
        out_shape=[
            jax.ShapeDtypeStruct((n * t, ATT_WIDTH), BF16),
            jax.ShapeDtypeStruct((n, past, kvw), F32),
        ],
        compiler_params=_cparams(("arbitrary",)),
        name="swa_sample",
    )(sinks, main, main, main, main, cache_k, cache_v, cos_t, sin_t)


def _out_proj_kernel(odn_ref, oat_ref, w_ref, x_ref, mod_ref, gpost_ref, *rest, tt, nk, emit_h):
    if emit_h:
        modn_ref, gpre_ref, xo_ref, h_ref = rest
    else:
        (xo_ref,) = rest
    kk = pl.program_id(1)
    half = nk // 2

    @pl.when(kk == 0)
    def _():
        xo_ref[...] = _dot(odn_ref[...], w_ref[...])

    @pl.when((kk > 0) & (kk < half))
    def _():
        xo_ref[...] += _dot(odn_ref[...], w_ref[...])

    @pl.when(kk >= half)
    def _():
        xo_ref[...] += _dot(oat_ref[...], w_ref[...])

    @pl.when(kk == nk - 1)
    def _():
        def body(chunks):
            ys = [xo_ref[rows, :] for rows, _ in chunks]
            x_new = [x_ref[rows, :] + mod_ref[b, 2] * (y * r * gpost_ref[...])
                     for (rows, b), y, r in zip(chunks, ys, _inv_rms(ys))]
            for (rows, _), xn in zip(chunks, x_new):
                xo_ref[rows, :] = xn
            if emit_h:
                for (rows, b), xn, r in zip(chunks, x_new, _inv_rms(x_new)):
                    h_ref[rows, :] = (xn * r * gpre_ref[...] * (1.0 + modn_ref[b, 1])
                                      + modn_ref[b, 0]).astype(BF16)

        _row_chunks(xo_ref.shape[0], tt, body)


def _out_proj(o_dn, o_at, w_out_b, layer, x2, mod, g_post, mod_next, g_pre_next, *, nb, tt, tk):
    m, d = x2.shape
    tm = nb * tt
    nk = (DN_WIDTH + ATT_WIDTH) // tk
    half = nk // 2
    tiles_per_batch = (m // mod.shape[0]) // tt
    emit_h = mod_next is not None
    mod_spec = pl.BlockSpec((nb, 3, ROW_CHUNK, d), lambda i, k: (i // tiles_per_batch, 0, 0, 0))
    vec_spec = pl.BlockSpec((ROW_CHUNK, d), lambda i, k: (0, 0))
    row_spec = pl.BlockSpec((tm, d), lambda i, k: (i, 0))
    in_specs = [
        pl.BlockSpec((tm, tk), lambda i, k: (i, jnp.minimum(k, half - 1))),
        pl.BlockSpec((tm, tk), lambda i, k: (i, jnp.maximum(k - half, 0))),
        pl.BlockSpec((None, tk, d), lambda i, k: (layer, k, 0)),
        row_spec, mod_spec, vec_spec,
    ]
    args = [o_dn, o_at, w_out_b, x2, mod, _rows_bcast(g_post)]
    out_specs = [row_spec]
    out_shape = [jax.ShapeDtypeStruct((m, d), F32)]
    if emit_h:
        in_specs += [mod_spec, vec_spec]
        args += [mod_next, _rows_bcast(g_pre_next)]
        out_specs.append(row_spec)
        out_shape.append(jax.ShapeDtypeStruct((m, d), BF16))
    outs = pl.pallas_call(
        functools.partial(_out_proj_kernel, tt=tt, nk=nk, emit_h=emit_h),
        grid=(m // tm, nk),
        in_specs=in_specs,
        out_specs=out_specs,
        out_shape=out_shape,
        compiler_params=_cparams(("arbitrary", "arbitrary")),
        name="out_proj",
    )(*args)
    return (outs[0], outs[1]) if emit_h else (outs[0], None)


def _rope_tables(pos):
    half = ATT_HEAD_DIM // 2
    inv = ROPE_THETA ** (-jnp.arange(half, dtype=F32) / half)
    ang = pos.astype(F32)[:, None] * inv[None, :]
    cos = jnp.cos(ang)
    sin = jnp.sin(ang)
    reps = LANES // ATT_HEAD_DIM
    return jnp.tile(jnp.concatenate([cos, cos], axis=-1), (1, reps)), \
        jnp.tile(jnp.concatenate([-sin, sin], axis=-1), (1, reps))


def _layer(x2, h, mod, mod_next, conv_buf, s0, cache_k, cache_v, pos, lw, layer, *, n, t, prompt):
    (g_post, g_pre_next, w_dn, w_at, w_ba, conv_w, gpar, dn_norm, sinks, w_out_b) = lw
    if prompt:
        nb, tt, tm_in, cs = 1, 512, min(1024, n * t), CHUNK
    else:
        nb, tt, tm_in, cs = n, t, n * t, min(CHUNK, t)
    (main_dn,) = _in_proj(h, w_dn, layer, tm=tm_in, tn=1024)
    main, ba = _in_proj(h, w_at, layer, tm=tm_in, tn=768, w_ba=w_ba)

    conv_buf8 = jnp.pad(conv_buf, ((0, 0), (SUBLANES - (CONV_W - 1), 0), (0, 0)))
    main_dn3 = main_dn.reshape(n, t, DN_PART_DIM)
    o_dn, s_new = _deltanet(main_dn3, ba.reshape(n, t, BA_DIM), conv_buf8, conv_w, gpar, dn_norm, s0,
                            cs=cs, nbk=DN_SEQS_PER_STEP)
    o_dn = o_dn.reshape(n * t, DN_WIDTH)
    conv_new = main_dn3[:, t - (CONV_W - 1):, COL_QKV:COL_QKV + DN_CONV_DIM]

    cos_t, sin_t = _rope_tables(pos)
    v_rows = main.reshape(n, t, AT_PART_DIM)[:, :, COL_VAT:COL_VAT + ATT_KV_WIDTH]
    if prompt:
        o_at, k_new = _swa_prompt(main, sinks, cos_t, sin_t, n=n, t=t)
        v_new = v_rows[:, t - WINDOW:]
    else:
        past = cache_k.shape[1]
        ck = cache_k.reshape(n, past, ATT_KV_WIDTH)
        cv = cache_v.reshape(n, past, ATT_KV_WIDTH)
        o_at, k_new = _swa_sample(main, sinks, ck, cv, cos_t, sin_t, n=n, t=t)
        v_new = jnp.concatenate([cv, v_rows], axis=1)[:, -past:]
    rows = k_new.shape[1]
    k_new = k_new.reshape(n, rows, ATT_KV_HEADS, ATT_HEAD_DIM)
    v_new = v_new.reshape(n, rows, ATT_KV_HEADS, ATT_HEAD_DIM)

    x_new, h_next = _out_proj(o_dn, o_at, w_out_b, layer, x2, mod, g_post, mod_next, g_pre_next,
                              nb=nb, tt=tt, tk=512)
    return x_new, h_next, conv_new, s_new, k_new, v_new


def kernel(x_prompt, x_sample, state_conv, state_dn, cache_k, cache_v, c_prompt, c_sample,
           w_ada, b_ada, g_pre, g_post, w_in, conv_w, a_log, dt_bias, dn_norm, sinks, w_out):
    depth = w_in.shape[0]
    bp, tp, d = x_prompt.shape
    bs, ts, _ = x_sample.shape
    pos_p = jnp.arange(tp)
    pos_s = PAST_LEN + jnp.arange(ts)

    w_dn, w_at, w_ba = _w_prep(w_in)
    w_out_b = w_out.astype(BF16)
    pad_rows = (-(bp + bs)) % (2 * SUBLANES)
    c_all = jnp.concatenate([c_prompt, c_sample, jnp.zeros((pad_rows, d), F32)], axis=0)
    gate_pad = jnp.zeros((depth, BA_DIM - 2 * DN_HEADS), F32)
    head_pad = jnp.zeros((depth, DN_HEADS), F32)
    gpar = jnp.stack([jnp.concatenate([head_pad, a_log, gate_pad], axis=-1),
                      jnp.concatenate([head_pad, dt_bias, gate_pad], axis=-1)], axis=1)

    zero_conv = jnp.zeros((bp, CONV_W - 1, DN_CONV_DIM), F32)
    zero_state = jnp.zeros((bp, DN_HEADS, DN_HEAD_DIM, DN_HEAD_DIM), F32)

    ada = _ada(c_all, w_ada, b_ada)
    mod_p = _rows_bcast(ada[:, :bp].reshape(depth, bp, 3, d))
    mod_s = _rows_bcast(ada[:, bp:bp + bs].reshape(depth, bs, 3, d))

    xp = x_prompt.reshape(bp * tp, d)
    xs = x_sample.reshape(bs * ts, d)
    hp = _prenorm(xp, mod_p[0], g_pre[0], nb=1, tt=512)
    hs = _prenorm(xs, mod_s[0], g_pre[0], nb=bs, tt=ts)
    outs_p, outs_s = [], []
    for l in range(depth):
        last = l == depth - 1
        lw = (g_post[l], None if last else g_pre[l + 1], w_dn, w_at, w_ba, conv_w[l], gpar[l], dn_norm[l],
              sinks[l], w_out_b)
        xp, hp, *rest_p = _layer(xp, hp, mod_p[l], None if last else mod_p[l + 1], zero_conv, zero_state,
                                 None, None, pos_p, lw, l, n=bp, t=tp, prompt=True)
        xs, hs, *rest_s = _layer(xs, hs, mod_s[l], None if last else mod_s[l + 1], state_conv[l], state_dn[l],
                                 cache_k[l], cache_v[l], pos_s, lw, l, n=bs, t=ts, prompt=False)
        outs_p.append(rest_p)
        outs_s.append(rest_s)
    xp = xp.reshape(bp, tp, d)
    xs = xs.reshape(bs, ts, d)

    def stack(outs, i):
        return jnp.stack([o[i] for o in outs])

    return (xp, xs,
            stack(outs_p, 0), stack(outs_p, 1), stack(outs_p, 2), stack(outs_p, 3),
            stack(outs_s, 0), stack(outs_s, 1), stack(outs_s, 2), stack(outs_s, 3))
```
